```python
import math
import jax, jax.numpy as jnp
from jax import lax
import numpy as np

D_MODEL = 4096
BATCH = 8
SEQ = 2048
DEPTH = 2

GRID_W = 64
CTX_LEN = 256
N_MIXERS = 2
GROUP = 16
N_GROUPS = D_MODEL // GROUP
STATE = 64
CONV_W = 3
D_FF = (((8 * D_MODEL + 2) // 3 + 255) // 256) * 256
LN_EPS = 1e-5
ALPHA = (2.0 * DEPTH) ** 0.25
BETA = (8.0 * DEPTH) ** -0.25
N_LAYERS_A = (DEPTH + 1) // 2
N_LAYERS_B = DEPTH // 2
LOG_DT_MIN = math.log(1e-3)
LOG_DT_MAX = math.log(1e-1)

kernel_name = "hybrid_s5_shortconv_flow_backbone"


def _layer_norm(x, g, b):
    xf = x.astype(jnp.float32)
    mu = jnp.mean(xf, axis=-1, keepdims=True)
    var = jnp.mean(jnp.square(xf - mu), axis=-1, keepdims=True)
    return ((xf - mu) * lax.rsqrt(var + LN_EPS)).astype(x.dtype) * g + b


def _modulate(x, shift, scale):
    return x * (1.0 + scale) + shift


def _swiglu(h, w1, w3, w2):
    return (jax.nn.silu(h @ w1) * (h @ w3)) @ w2


def _zoh(lam_re, lam_im, log_step, b_re, b_im):
    dt = jnp.exp(log_step)[:, None]
    mag = jnp.exp(lam_re * dt)
    ang = lam_im * dt
    a_re = mag * jnp.cos(ang)
    a_im = mag * jnp.sin(ang)
    n_re, n_im = a_re - 1.0, a_im
    den = lam_re * lam_re + lam_im * lam_im
    f_re = (n_re * lam_re + n_im * lam_im) / den
    f_im = (n_im * lam_re - n_re * lam_im) / den
    bb_re = f_re[..., None] * b_re - f_im[..., None] * b_im
    bb_im = f_re[..., None] * b_im + f_im[..., None] * b_re
    return a_re, a_im, bb_re, bb_im


def _complex_scan(a_re, a_im, b_re, b_im):
    n = b_re.shape[0]
    a_re = jnp.broadcast_to(a_re, (n, 1) + a_re.shape)
    a_im = jnp.broadcast_to(a_im, (n, 1) + a_im.shape)

    def combine(e1, e2):
        a1r, a1i, b1r, b1i = e1
        a2r, a2i, b2r, b2i = e2
        return (a2r * a1r - a2i * a1i,
                a2r * a1i + a2i * a1r,
                a2r * b1r - a2i * b1i + b2r,
                a2r * b1i + a2i * b1r + b2i)

    _, _, s_re, s_im = lax.associative_scan(combine, (a_re, a_im, b_re, b_im), axis=0)
    return s_re, s_im


def _readout(s_re, s_im, c_re, c_im):
    return (jnp.einsum('nbgp,ghp->bngh', s_re, c_re)
            - jnp.einsum('nbgp,ghp->bngh', s_im, c_im))


def _s5_direction(ug_c, ug_l, lam_re, lam_im, log_step, b_re, b_im, c_re, c_im, ctx_out):
    a_re, a_im, bb_re, bb_im = _zoh(lam_re, lam_im, log_step, b_re, b_im)
    bc_re = jnp.einsum('bngh,gph->nbgp', ug_c, bb_re)
    bc_im = jnp.einsum('bngh,gph->nbgp', ug_c, bb_im)
    sc_re, sc_im = _complex_scan(a_re, a_im, bc_re, bc_im)
    h_re, h_im = sc_re[-1], sc_im[-1]
    bl_re = jnp.einsum('bngh,gph->nbgp', ug_l, bb_re)
    bl_im = jnp.einsum('bngh,gph->nbgp', ug_l, bb_im)
    bl_re = bl_re.at[0].add(a_re * h_re - a_im * h_im)
    bl_im = bl_im.at[0].add(a_re * h_im + a_im * h_re)
    sl_re, sl_im = _complex_scan(a_re, a_im, bl_re, bl_im)
    y_l = _readout(sl_re, sl_im, c_re, c_im)
    y_c = _readout(sc_re, sc_im, c_re, c_im) if ctx_out else None
    return y_c, y_l


def _glu_out(y, w_glu, w_out):
    z = jax.nn.gelu(y)
    return (z * jax.nn.sigmoid(z @ w_glu)) @ w_out


def _s5_mixer(h_c, h_l, w_in, lam_re, lam_im, log_step, b_re, b_im, c_re, c_im,
              d_skip, w_glu, w_out, ctx_out):
    bsz, n_lat, _ = h_l.shape
    n_ctx = h_c.shape[1]
    u_l = h_l @ w_in
    u_c = h_c @ w_in
    ug_l = u_l.reshape(bsz, n_lat, N_GROUPS, GROUP)
    ug_c = u_c.reshape(bsz, n_ctx, N_GROUPS, GROUP)
    y_l = d_skip * u_l
    y_c = d_skip * u_c if ctx_out else None
    for direction in range(2):
        flip = (lambda t: t[:, ::-1]) if direction == 1 else (lambda t: t)
        yc_d, yl_d = _s5_direction(flip(ug_c), flip(ug_l), lam_re[direction], lam_im[direction],
                                   log_step[direction], b_re[direction], b_im[direction],
                                   c_re[direction], c_im[direction], ctx_out)
        y_l = y_l + flip(yl_d).reshape(bsz, n_lat, D_MODEL)
        if ctx_out:
            y_c = y_c + flip(yc_d).reshape(bsz, n_ctx, D_MODEL)
    out_l = _glu_out(y_l, w_glu, w_out)
    out_c = _glu_out(y_c, w_glu, w_out) if ctx_out else None
    return out_c, out_l


def _conv3(v, w):
    pad = [(0, 0)] * (v.ndim - 2) + [(1, 1), (0, 0)]
    vp = jnp.pad(v, pad)
    return w[0] * vp[..., :-2, :] + w[1] * vp[..., 1:-1, :] + w[2] * vp[..., 2:, :]


def _conv_mixer(h_c, h_l, w_in, conv_w, w_out):
    bsz, n_lat, _ = h_l.shape
    rows = n_lat // GRID_W

    def gated(h):
        g_b, g_c, v = jnp.split(h @ w_in, 3, axis=-1)
        return g_b, g_c * v

    gb_l, z_l = gated(h_l)
    z_l = _conv3(z_l.reshape(bsz, rows, GRID_W, D_MODEL), conv_w).reshape(bsz, n_lat, D_MODEL)
    out_l = (gb_l * z_l) @ w_out
    out_c = None
    if h_c is not None:
        gb_c, z_c = gated(h_c)
        out_c = (gb_c * _conv3(z_c, conv_w)) @ w_out
    return out_c, out_l


def setup_inputs(seed: int = 0) -> dict:
    key = jax.random.key(seed)
    ks = jax.random.split(key, 26)
    D, G, P, H = D_MODEL, N_GROUPS, STATE, GROUP

    def nrm(k, shape, s):
        return jax.random.normal(k, shape, jnp.float32) * s

    x = nrm(ks[0], (BATCH, SEQ, D), 1.0)
    c = nrm(ks[1], (BATCH, D), 1.0)
    ctx = nrm(ks[2], (BATCH, CTX_LEN, D), 1.0)
    c_ctx = nrm(ks[3], (D,), 1.0)
    w_mod = nrm(ks[4], (DEPTH, D, 6 * D), 0.5 * D ** -0.5)
    b_mod = nrm(ks[5], (DEPTH, 6 * D), 0.02)
    ln_g = 1.0 + nrm(ks[6], (DEPTH, 2, D), 0.02)
    ln_b = nrm(ks[7], (DEPTH, 2, D), 0.02)
    w_in_a = nrm(ks[8], (N_LAYERS_A, D, D), D ** -0.5)
    lam_re = -0.5 + nrm(ks[9], (N_LAYERS_A, 2, G, P), 0.01)
    lam_im = jnp.pi * jnp.arange(P, dtype=jnp.float32) + nrm(ks[10], (N_LAYERS_A, 2, G, P), 0.01)
    log_step = jax.random.uniform(ks[11], (N_LAYERS_A, 2, G), jnp.float32, LOG_DT_MIN, LOG_DT_MAX)
    b_re = nrm(ks[12], (N_LAYERS_A, 2, G, P, H), (2.0 * H) ** -0.5)
    b_im = nrm(ks[13], (N_LAYERS_A, 2, G, P, H), (2.0 * H) ** -0.5)
    c_re = nrm(ks[14], (N_LAYERS_A, 2, G, H, P), P ** -0.5)
    c_im = nrm(ks[15], (N_LAYERS_A, 2, G, H, P), P ** -0.5)
    d_skip = nrm(ks[16], (N_LAYERS_A, D), 1.0)
    w_glu_a = nrm(ks[17], (N_LAYERS_A, D, D), D ** -0.5)
    w_out_a = nrm(ks[18], (N_LAYERS_A, D, D), BETA * D ** -0.5)
    w_in_b = nrm(ks[19], (N_LAYERS_B, D, 3 * D), D ** -0.5)
    conv_w = nrm(ks[20], (N_LAYERS_B, CONV_W, D), CONV_W ** -0.5)
    w_out_b = nrm(ks[21], (N_LAYERS_B, D, D), BETA * D ** -0.5)
    w1 = nrm(ks[22], (DEPTH, D, D_FF), D ** -0.5)
    w3 = nrm(ks[23], (DEPTH, D, D_FF), D ** -0.5)
    w2 = nrm(ks[24], (DEPTH, D_FF, D), BETA * D_FF ** -0.5)
    return {"x": x, "c": c, "ctx": ctx, "c_ctx": c_ctx, "w_mod": w_mod, "b_mod": b_mod,
            "ln_g": ln_g, "ln_b": ln_b, "w_in_a": w_in_a, "lam_re": lam_re, "lam_im": lam_im,
            "log_step": log_step, "b_re": b_re, "b_im": b_im, "c_re": c_re, "c_im": c_im,
            "d_skip": d_skip, "w_glu_a": w_glu_a, "w_out_a": w_out_a, "w_in_b": w_in_b,
            "conv_w": conv_w, "w_out_b": w_out_b, "w1": w1, "w3": w3, "w2": w2}


def reference(x, c, ctx, c_ctx, w_mod, b_mod, ln_g, ln_b, w_in_a, lam_re, lam_im, log_step,
              b_re, b_im, c_re, c_im, d_skip, w_glu_a, w_out_a, w_in_b, conv_w, w_out_b,
              w1, w3, w2):
    xl, xc = x, ctx
    silu_c = jax.nn.silu(c)
    silu_cc = jax.nn.silu(c_ctx)
    for i in range(DEPTH):
        kind = i % N_MIXERS
        j = i // N_MIXERS
        ctx_later = any(k % N_MIXERS == 0 for k in range(i + 1, DEPTH))
        mod = (silu_c @ w_mod[i] + b_mod[i])[:, None, :]
        sh_m, sc_m, g_m, sh_f, sc_f, g_f = jnp.split(mod, 6, axis=-1)
        h_l = _modulate(xl, sh_m, sc_m)
        h_c = None
        if kind == 0 or ctx_later:
            mod_c = silu_cc @ w_mod[i] + b_mod[i]
            csh_m, csc_m, cg_m, csh_f, csc_f, cg_f = jnp.split(mod_c, 6, axis=-1)
            h_c = _modulate(xc, csh_m, csc_m)
        if kind == 0:
            o_c, o_l = _s5_mixer(h_c, h_l, w_in_a[j], lam_re[j], lam_im[j], log_step[j],
                                 b_re[j], b_im[j], c_re[j], c_im[j], d_skip[j],
                                 w_glu_a[j], w_out_a[j], ctx_later)
        else:
            o_c, o_l = _conv_mixer(h_c if ctx_later else None, h_l, w_in_b[j], conv_w[j], w_out_b[j])
        xl = _layer_norm(ALPHA * xl + g_m * o_l, ln_g[i, 0], ln_b[i, 0])
        xl = _layer_norm(ALPHA * xl + g_f * _swiglu(_modulate(xl, sh_f, sc_f), w1[i], w3[i], w2[i]),
                         ln_g[i, 1], ln_b[i, 1])
        if ctx_later:
            xc = _layer_norm(ALPHA * xc + cg_m * o_c, ln_g[i, 0], ln_b[i, 0])
            xc = _layer_norm(ALPHA * xc + cg_f * _swiglu(_modulate(xc, csh_f, csc_f), w1[i], w3[i], w2[i]),
                             ln_g[i, 1], ln_b[i, 1])
    return xl
```

```python
import functools
import math

import jax
import jax.numpy as jnp
from jax import lax
from jax.experimental import pallas as pl
from jax.experimental.pallas import tpu as pltpu

F32 = jnp.float32
BF16 = jnp.bfloat16

D_MODEL = 4096
BATCH = 8
SEQ = 2048
CTX_LEN = 256
GROUP = 16
STATE = 64
GRID_W = 64
DEPTH = 2
LN_EPS = 1e-5
ALPHA = (2.0 * DEPTH) ** 0.25

LANES = 128
SUBLANES = 8
VMEM_CAP_BYTES = 64 * 1024 * 1024
VMEM_CEILING_BYTES = 58 * 1024 * 1024

T_CHUNK = 16
N_SLAB = D_MODEL // LANES
G_SLAB = LANES // GROUP
SP = G_SLAB * STATE
J_LAT = SEQ // T_CHUNK
J_CTX = CTX_LEN // T_CHUNK
JB = 32
N_JB = J_LAT // JB
KT = T_CHUNK * LANES


def _vmem_limit(nbytes):
    return int(min(VMEM_CEILING_BYTES, max(32 * 1024 * 1024, nbytes)))


def _cparams(sem, nbytes):
    return pltpu.CompilerParams(dimension_semantics=sem, vmem_limit_bytes=_vmem_limit(nbytes))


def _mod_kernel(c_ref, w_ref, b_ref, o_ref):
    c = c_ref[...]
    s = (c * jax.nn.sigmoid(c)).astype(BF16)
    o_ref[...] = jnp.dot(s, w_ref[...].astype(BF16), preferred_element_type=F32) + b_ref[...]


def _mod_rows(c_rows, w, b):
    m, k = c_rows.shape
    n = w.shape[1]
    tn = 512
    return pl.pallas_call(
        _mod_kernel,
        grid=(n // tn,),
        in_specs=[pl.BlockSpec((m, k), lambda j: (0, 0)),
                  pl.BlockSpec((k, tn), lambda j: (0, j)),
                  pl.BlockSpec((1, tn), lambda j: (0, j))],
        out_specs=pl.BlockSpec((m, tn), lambda j: (0, j)),
        out_shape=jax.ShapeDtypeStruct((m, n), F32),
        compiler_params=_cparams(("arbitrary",), 3 * k * tn * 4),
        name="mod_rows",
    )(c_rows, w, b.reshape(1, n))


def _modulate_kernel(x_ref, sh_ref, sc_ref, o_ref):
    o_ref[...] = (x_ref[...] * (1.0 + sc_ref[...]) + sh_ref[...]).astype(o_ref.dtype)


def _modulate(x, shift, scale, tr=512):
    b, n, d = x.shape
    tr = min(tr, n)
    per = n // tr
    return pl.pallas_call(
        _modulate_kernel,
        grid=(b * per,),
        in_specs=[pl.BlockSpec((None, tr, d), lambda i: (i // per, i % per, 0)),
                  pl.BlockSpec((None, 1, d), lambda i: (i // per, 0, 0)),
                  pl.BlockSpec((None, 1, d), lambda i: (i // per, 0, 0))],
        out_specs=pl.BlockSpec((tr, d), lambda i: (i, 0)),
        out_shape=jax.ShapeDtypeStruct((b * n, d), BF16),
        compiler_params=_cparams(("arbitrary",), 2 * tr * d * 6 + (8 << 20)),
        name="modulate",
    )(x, shift, scale)


def _resid_ln_kernel(*refs, with_h):
    if with_h:
        xl_ref, o_ref, g_ref, lg_ref, lb_ref, sh_ref, sc_ref, xo_ref, ho_ref = refs
    else:
        xl_ref, o_ref, g_ref, lg_ref, lb_ref, xo_ref = refs
    y = ALPHA * xl_ref[...] + g_ref[...] * o_ref[...]
    mu = jnp.mean(y, axis=-1, keepdims=True)
    yc = y - mu
    var = jnp.mean(yc * yc, axis=-1, keepdims=True)
    xn = yc * lax.rsqrt(var + LN_EPS) * lg_ref[...] + lb_ref[...]
    xo_ref[...] = xn
    if with_h:
        ho_ref[...] = (xn * (1.0 + sc_ref[...]) + sh_ref[...]).astype(ho_ref.dtype)


def _resid_ln(xl, o, gate, ln_g, ln_b, shift=None, scale=None, tr=256):
    m, d = xl.shape
    per = SEQ // tr
    with_h = shift is not None
    row = pl.BlockSpec((tr, d), lambda i: (i, 0))
    bvec = pl.BlockSpec((None, 1, d), lambda i: (i // per, 0, 0))
    vec = pl.BlockSpec((1, d), lambda i: (0, 0))
    in_specs = [row, row, bvec, vec, vec]
    args = [xl, o, gate, ln_g.reshape(1, d), ln_b.reshape(1, d)]
    out_specs = [row]
    out_shape = [jax.ShapeDtypeStruct((m, d), F32)]
    if with_h:
        in_specs += [bvec, bvec]
        args += [shift, scale]
        out_specs.append(row)
        out_shape.append(jax.ShapeDtypeStruct((m, d), BF16))
    res = pl.pallas_call(
        functools.partial(_resid_ln_kernel, with_h=with_h),
        grid=(m // tr,),
        in_specs=in_specs, out_specs=out_specs, out_shape=out_shape,
        compiler_params=_cparams(("arbitrary",), 2 * tr * d * 14 + (12 << 20)),
        name="resid_ln",
    )(*args)
    return res if with_h else res[0]


def _mm_kernel(*refs, n_w, epilogue, chunked_lhs, chunked_out, tn):
    lhs_ref = refs[0]
    w_refs = refs[1:1 + n_w]
    extra = refs[1 + n_w:-1]
    o_ref = refs[-1]
    lhs = lhs_ref[...]
    if chunked_lhs:
        lhs = lhs.reshape(lhs.shape[0] * lhs.shape[1], lhs.shape[2])
    lhs_b = lhs.astype(BF16)
    accs = [jnp.dot(lhs_b, w[...].astype(BF16), preferred_element_type=F32) for w in w_refs]
    if epilogue == "plain":
        out = accs[0]
    elif epilogue == "glu":
        j = pl.program_id(1)
        zt = lhs_ref[:, :, pl.ds(pl.multiple_of(j * tn, tn), tn)]
        zt = zt.reshape(zt.shape[0] * zt.shape[1], tn)
        out = zt * jax.nn.sigmoid(accs[0])
    elif epilogue == "swiglu":
        a1, a3 = accs
        out = a1 * jax.nn.sigmoid(a1) * a3
    elif epilogue == "conv":
        gb, gc, v = accs
        cw_ref, = extra
        t = gc * v
        tm = t.shape[0]
        pos = lax.broadcasted_iota(jnp.int32, t.shape, 0) % GRID_W
        prev = jnp.where(pos == 0, 0.0, pltpu.roll(t, 1, 0))
        nxt = jnp.where(pos == GRID_W - 1, 0.0, pltpu.roll(t, tm - 1, 0))
        out = gb * (cw_ref[0:1, :] * prev + cw_ref[1:2, :] * t + cw_ref[2:3, :] * nxt)
    else:
        raise ValueError(epilogue)
    out = out.astype(o_ref.dtype)
    if chunked_out:
        out = out.reshape(o_ref.shape)
    o_ref[...] = out


def _matmul(lhs, weights, w_col_offsets, n_out, *, tm, tn, epilogue, out_dtype,
            chunked_lhs=False, chunked_out=False, extra=(), extra_specs=(), single_buffer_lhs=False,
            name="matmul"):
    k = weights[0].shape[0]
    if chunked_lhs:
        jn, bn, tt, _ = lhs.shape
        m = jn * bn * tt
        per = (jn * tt) // tm
        lhs_spec = pl.BlockSpec((tm // tt, None, tt, k), lambda i, j: (i % per, i // per, 0, 0))
    else:
        m = lhs.shape[0]
        kw = dict(pipeline_mode=pl.Buffered(1)) if single_buffer_lhs else {}
        lhs_spec = pl.BlockSpec((tm, k), lambda i, j: (i, 0), **kw)
    w_specs = [pl.BlockSpec((k, tn), functools.partial(lambda i, j, off: (0, j + off), off=off // tn))
               for off in w_col_offsets]
    if chunked_out:
        jn = (m // BATCH) // T_CHUNK
        per = (jn * T_CHUNK) // tm
        out_spec = pl.BlockSpec((tm // T_CHUNK, None, T_CHUNK, tn), lambda i, j: (i % per, i // per, 0, j))
        out_shape = jax.ShapeDtypeStruct((jn, BATCH, T_CHUNK, n_out), out_dtype)
    else:
        out_spec = pl.BlockSpec((tm, tn), lambda i, j: (i, j))
        out_shape = jax.ShapeDtypeStruct((m, n_out), out_dtype)
    lhs_bytes = tm * k * lhs.dtype.itemsize * (1 if single_buffer_lhs else 2)
    w_bytes = sum(2 * k * tn * w.dtype.itemsize + k * tn * 2 for w in weights)
    out_bytes = 2 * tm * tn * 4 * (2 + len(weights))
    cast_bytes = tm * k * 2 if lhs.dtype != BF16 else 0
    return pl.pallas_call(
        functools.partial(_mm_kernel, n_w=len(weights), epilogue=epilogue, chunked_lhs=chunked_lhs,
                          chunked_out=chunked_out, tn=tn),
        grid=(m // tm, n_out // tn),
        in_specs=[lhs_spec] + w_specs + list(extra_specs),
        out_specs=out_spec, out_shape=out_shape,
        compiler_params=_cparams(("arbitrary", "arbitrary"),
                                 lhs_bytes + w_bytes + out_bytes + cast_bytes + (4 << 20)),
        name=name,
    )(lhs, *weights, *extra)


def _s5_prep_kernel(lr_ref, li_ref, ls_ref, bre_ref, bim_ref, cre_ref, cim_ref,
                    tall_ref, w_ref, v_ref, at_ref, *, reverse):
    lr = lr_ref[...]
    li = li_ref[...]
    dt = jnp.exp(ls_ref[...])
    kk = jnp.minimum(lax.broadcasted_iota(jnp.int32, (LANES, SP), 0), T_CHUNK).astype(F32)
    mag = jnp.exp(kk * (lr * dt))
    ang = kk * (li * dt)
    p_re = mag * jnp.cos(ang)
    p_im = mag * jnp.sin(ang)
    a_re = p_re[1:2, :]
    a_im = p_im[1:2, :]
    n_re, n_im = a_re - 1.0, a_im
    den = lr * lr + li * li
    f_re = (n_re * lr + n_im * li) / den
    f_im = (n_im * lr - n_re * li) / den
    b_re = bre_ref[...]
    b_im = bim_ref[...]
    bb_re = f_re * b_re - f_im * b_im
    bb_im = f_re * b_im + f_im * b_re
    c_re = cre_ref[...]
    c_im = cim_ref[...]

    dks = []
    for k in range(T_CHUNK):
        pr, pi = p_re[k:k + 1, :], p_im[k:k + 1, :]
        bk_re = bb_re * pr - bb_im * pi
        bk_im = bb_re * pi + bb_im * pr
        dk = (jnp.dot(bk_re, c_re, preferred_element_type=F32, precision=lax.Precision.HIGHEST)
              - jnp.dot(bk_im, c_im, preferred_element_type=F32, precision=lax.Precision.HIGHEST))
        dks.append(dk.astype(BF16))
        m = k if reverse else T_CHUNK - 1 - k
        w_ref[m * LANES:(m + 1) * LANES, 0:SP] = bk_re.astype(BF16)
        w_ref[m * LANES:(m + 1) * LANES, SP:2 * SP] = bk_im.astype(BF16)

    zero = jnp.zeros((LANES, LANES), BF16)
    for r in range(T_CHUNK):
        for c in range(2):
            idx = (r - c) if reverse else (T_CHUNK - 2 - r + c)
            blk = dks[idx] if 0 <= idx < T_CHUNK else zero
            tall_ref[r * LANES:(r + 1) * LANES, c * LANES:(c + 1) * LANES] = blk

    pt_re = p_re.T
    pt_im = p_im.T
    for t in range(T_CHUNK):
        k = (T_CHUNK - t) if reverse else (t + 1)
        pr, pi = pt_re[:, k:k + 1], pt_im[:, k:k + 1]
        v_ref[0:SP, t * LANES:(t + 1) * LANES] = (pr * c_re - pi * c_im).astype(BF16)
        v_ref[SP:2 * SP, t * LANES:(t + 1) * LANES] = (-(pr * c_im + pi * c_re)).astype(BF16)

    at_ref[:, 0:SP] = p_re[T_CHUNK:T_CHUNK + 1, :]
    at_ref[:, SP:2 * SP] = p_im[T_CHUNK:T_CHUNK + 1, :]


def _s5_prep(lam_re, lam_im, log_step, b_re, b_im, c_re, c_im, *, reverse):
    def rows(v):
        return v.reshape(N_SLAB, 1, SP)

    eye = jnp.eye(G_SLAB, dtype=bool)

    def b_blockdiag(b):
        bt = b.reshape(N_SLAB, G_SLAB, STATE, GROUP).transpose(0, 1, 3, 2)
        bd = jnp.where(eye[None, :, None, :, None], bt[:, :, :, None, :], 0.0)
        return bd.reshape(N_SLAB, LANES, SP)

    def c_blockdiag(c):
        ct = c.reshape(N_SLAB, G_SLAB, GROUP, STATE).transpose(0, 1, 3, 2)
        cd = jnp.where(eye[None, :, None, :, None], ct[:, :, :, None, :], 0.0)
        return cd.reshape(N_SLAB, SP, LANES)

    ls = jnp.repeat(log_step, STATE).reshape(N_SLAB, 1, SP)
    vec = pl.BlockSpec((None, 1, SP), lambda s: (s, 0, 0))
    bsp = pl.BlockSpec((None, LANES, SP), lambda s: (s, 0, 0))
    csp = pl.BlockSpec((None, SP, LANES), lambda s: (s, 0, 0))
    return pl.pallas_call(
        functools.partial(_s5_prep_kernel, reverse=reverse),
        grid=(N_SLAB,),
        in_specs=[vec, vec, vec, bsp, bsp, csp, csp],
        out_specs=[pl.BlockSpec((None, KT, 2 * LANES), lambda s: (s, 0, 0)),
                   pl.BlockSpec((None, KT, 2 * SP), lambda s: (s, 0, 0)),
                   pl.BlockSpec((None, 2 * SP, KT), lambda s: (s, 0, 0)),
                   pl.BlockSpec((None, 1, 2 * SP), lambda s: (s, 0, 0))],
        out_shape=[jax.ShapeDtypeStruct((N_SLAB, KT, 2 * LANES), BF16),
                   jax.ShapeDtypeStruct((N_SLAB, KT, 2 * SP), BF16),
                   jax.ShapeDtypeStruct((N_SLAB, 2 * SP, KT), BF16),
                   jax.ShapeDtypeStruct((N_SLAB, 1, 2 * SP), F32)],
        compiler_params=_cparams(("arbitrary",), 40 << 20),
        name="s5_prep_bwd" if reverse else "s5_prep_fwd",
    )(rows(lam_re), rows(lam_im), ls, b_blockdiag(b_re), b_blockdiag(b_im),
      c_blockdiag(c_re), c_blockdiag(c_im))


def _s5_scan_kernel(*refs, reverse, second):
    it = iter(refs)
    ul_ref = next(it)
    uc_ref = next(it)
    y1_ref = next(it) if second else None
    tall_ref, w_ref, v_ref, at_ref = next(it), next(it), next(it), next(it)
    dsk_ref = None if second else next(it)
    o_ref = next(it)
    x2_ref, xc2_ref, xst_ref, sprev_ref, state_ref = it

    rows = JB * BATCH
    crow = J_CTX * BATCH
    a_re = at_ref[:, 0:SP]
    a_im = at_ref[:, SP:2 * SP]

    def token(ref, t, n):
        return ref[pl.ds(t, n, stride=T_CHUNK), :]

    def scan(n_chunks, keep):
        s_re = state_ref[:, 0:SP]
        s_im = state_ref[:, SP:2 * SP]
        order = range(n_chunks - 1, -1, -1) if reverse else range(n_chunks)
        for jj in order:
            r = slice(jj * BATCH, (jj + 1) * BATCH)
            if keep:
                sprev_ref[r, 0:SP] = s_re
                sprev_ref[r, SP:2 * SP] = s_im
            x_re = xst_ref[r, 0:SP]
            x_im = xst_ref[r, SP:2 * SP]
            s_re, s_im = (a_re * s_re - a_im * s_im + x_re,
                          a_re * s_im + a_im * s_re + x_im)
        state_ref[:, 0:SP] = s_re
        state_ref[:, SP:2 * SP] = s_im

    @pl.when(pl.program_id(1) == 0)
    def _context():
        state_ref[...] = jnp.zeros_like(state_ref)
        for t in range(T_CHUNK):
            xc2_ref[:, t * LANES:(t + 1) * LANES] = token(uc_ref, t, crow).astype(BF16)
        xst_ref[0:crow, :] = jnp.dot(xc2_ref[...], w_ref[...], preferred_element_type=F32)
        scan(J_CTX, keep=False)

    for t in range(T_CHUNK):
        x2_ref[:, t * LANES:(t + 1) * LANES] = token(ul_ref, t, rows).astype(BF16)
    xst_ref[...] = jnp.dot(x2_ref[...], w_ref[...], preferred_element_type=F32)
    scan(JB, keep=True)
    sp = sprev_ref[...].astype(BF16)

    n_tiles = KT // (2 * LANES)
    for nt in range(n_tiles):
        cols = slice(nt * 2 * LANES, (nt + 1) * 2 * LANES)
        if reverse:
            xa = x2_ref[:, nt * 2 * LANES:]
            ta = tall_ref[0:(n_tiles - nt) * 2 * LANES, :]
        else:
            xa = x2_ref[:, 0:(nt + 1) * 2 * LANES]
            ta = tall_ref[(n_tiles - 1 - nt) * 2 * LANES:, :]
        y = (jnp.dot(xa, ta, preferred_element_type=F32)
             + jnp.dot(sp, v_ref[:, cols], preferred_element_type=F32))
        for c in range(2):
            t = 2 * nt + c
            yt = y[:, c * LANES:(c + 1) * LANES]
            if second:
                val = jax.nn.gelu(token(y1_ref, t, rows) + yt)
            else:
                val = dsk_ref[...] * token(ul_ref, t, rows) + yt
            o_ref[pl.ds(t, rows, stride=T_CHUNK), :] = val


def _s5_scan(u_lat, u_ctx, ops, *, reverse, y_prev=None, d_skip=None):
    tall, w, v, at = ops
    second = y_prev is not None
    blk_rows = JB * BATCH * T_CHUNK
    ctx_rows = J_CTX * BATCH * T_CHUNK

    def jblk(q):
        return (N_JB - 1 - q) if reverse else q

    lat_spec = pl.BlockSpec((blk_rows, LANES), lambda s, q: (jblk(q), s))
    in_specs = [lat_spec, pl.BlockSpec((ctx_rows, LANES), lambda s, q: (0, s))]
    args = [u_lat.reshape(J_LAT * BATCH * T_CHUNK, D_MODEL), u_ctx.reshape(ctx_rows, D_MODEL)]
    if second:
        in_specs.append(lat_spec)
        args.append(y_prev.reshape(J_LAT * BATCH * T_CHUNK, D_MODEL))
    in_specs += [pl.BlockSpec((None, KT, 2 * LANES), lambda s, q: (s, 0, 0)),
                 pl.BlockSpec((None, KT, 2 * SP), lambda s, q: (s, 0, 0)),
                 pl.BlockSpec((None, 2 * SP, KT), lambda s, q: (s, 0, 0)),
                 pl.BlockSpec((None, 1, 2 * SP), lambda s, q: (s, 0, 0))]
    args += [tall, w, v, at]
    if not second:
        in_specs.append(pl.BlockSpec((None, 1, LANES), lambda s, q: (s, 0, 0)))
        args.append(d_skip.reshape(N_SLAB, 1, LANES))
    rows = JB * BATCH
    out = pl.pallas_call(
        functools.partial(_s5_scan_kernel, reverse=reverse, second=second),
        grid=(N_SLAB, N_JB),
        in_specs=in_specs,
        out_specs=lat_spec,
        out_shape=jax.ShapeDtypeStruct((J_LAT * BATCH * T_CHUNK, D_MODEL), F32),
        scratch_shapes=[pltpu.VMEM((rows, KT), BF16),
                        pltpu.VMEM((J_CTX * BATCH, KT), BF16),
                        pltpu.VMEM((rows, 2 * SP), F32),
                        pltpu.VMEM((rows, 2 * SP), F32),
                        pltpu.VMEM((BATCH, 2 * SP), F32)],
        compiler_params=_cparams(("arbitrary", "arbitrary"), 48 << 20),
        name="s5_scan_bwd" if reverse else "s5_scan_fwd",
    )(*args)
    return out.reshape(J_LAT, BATCH, T_CHUNK, D_MODEL)


def kernel(x, c, ctx, c_ctx, w_mod, b_mod, ln_g, ln_b, w_in_a, lam_re, lam_im, log_step,
           b_re, b_im, c_re, c_im, d_skip, w_glu_a, w_out_a, w_in_b, conv_w, w_out_b, w1, w3, w2):
    d = D_MODEL
    d_ff = w1.shape[-1]
    m = BATCH * SEQ

    c_rows = jnp.concatenate([c, c_ctx[None, :], jnp.zeros((16 - BATCH - 1, d), F32)], axis=0)
    mods = [_mod_rows(c_rows, w_mod[i], b_mod[i]).reshape(16, 6, d) for i in range(DEPTH)]

    def lat(i, k):
        return mods[i][:BATCH, k, :].reshape(BATCH, 1, d)

    def ctxrow(i, k):
        return jnp.broadcast_to(mods[i][BATCH, k, :].reshape(1, 1, d), (BATCH, 1, d))

    h_l = _modulate(x, lat(0, 0), lat(0, 1))
    h_c = _modulate(ctx, ctxrow(0, 0), ctxrow(0, 1))
    u_l = _matmul(h_l, [w_in_a[0]], [0], d, tm=1024, tn=256, epilogue="plain", out_dtype=F32,
                  chunked_out=True, name="mm_in_a")
    u_c = _matmul(h_c, [w_in_a[0]], [0], d, tm=CTX_LEN, tn=256, epilogue="plain", out_dtype=F32,
                  chunked_out=True, name="mm_in_a_ctx")
    ops_f = _s5_prep(lam_re[0, 0], lam_im[0, 0], log_step[0, 0], b_re[0, 0], b_im[0, 0],
                     c_re[0, 0], c_im[0, 0], reverse=False)
    ops_b = _s5_prep(lam_re[0, 1], lam_im[0, 1], log_step[0, 1], b_re[0, 1], b_im[0, 1],
                     c_re[0, 1], c_im[0, 1], reverse=True)
    y_f = _s5_scan(u_l, u_c, ops_f, reverse=False, d_skip=d_skip[0])
    z = _s5_scan(u_l, u_c, ops_b, reverse=True, y_prev=y_f)
    gz = _matmul(z, [w_glu_a[0]], [0], d, tm=512, tn=512, epilogue="glu", out_dtype=BF16,
                 chunked_lhs=True, name="mm_glu")
    o = _matmul(gz, [w_out_a[0]], [0], d, tm=1024, tn=256, epilogue="plain", out_dtype=F32,
                name="mm_out_a")
    xl = x.reshape(m, d)
    xl, h = _resid_ln(xl, o, lat(0, 2), ln_g[0, 0], ln_b[0, 0], lat(0, 3), lat(0, 4))
    g = _matmul(h, [w1[0], w3[0]], [0, 0], d_ff, tm=1024, tn=256, epilogue="swiglu", out_dtype=BF16,
                single_buffer_lhs=True, name="mm_swiglu")
    o = _matmul(g, [w2[0].astype(BF16)], [0], d, tm=1024, tn=256, epilogue="plain", out_dtype=F32,
                single_buffer_lhs=True, name="mm_w2")
    xl, h = _resid_ln(xl, o, lat(0, 5), ln_g[0, 1], ln_b[0, 1], lat(1, 0), lat(1, 1))

    cw_spec = pl.BlockSpec((3, 256), lambda i, j: (0, j))
    gz = _matmul(h, [w_in_b[0]] * 3, [0, d, 2 * d], d, tm=1024, tn=256, epilogue="conv", out_dtype=BF16,
                 extra=(conv_w[0],), extra_specs=(cw_spec,), single_buffer_lhs=True, name="mm_conv")
    o = _matmul(gz, [w_out_b[0]], [0], d, tm=1024, tn=256, epilogue="plain", out_dtype=F32,
                name="mm_out_b")
    xl, h = _resid_ln(xl, o, lat(1, 2), ln_g[1, 0], ln_b[1, 0], lat(1, 3), lat(1, 4))
    g = _matmul(h, [w1[1], w3[1]], [0, 0], d_ff, tm=1024, tn=256, epilogue="swiglu", out_dtype=BF16,
                single_buffer_lhs=True, name="mm_swiglu")
    o = _matmul(g, [w2[1].astype(BF16)], [0], d, tm=1024, tn=256, epilogue="plain", out_dtype=F32,
                single_buffer_lhs=True, name="mm_w2")
    xl = _resid_ln(xl, o, lat(1, 5), ln_g[1, 1], ln_b[1, 1])
    return xl.reshape(BATCH, SEQ, d)
```

```python
import functools

import jax
import jax.numpy as jnp
from jax import lax
from jax.experimental import pallas as pl
from jax.experimental.pallas import tpu as pltpu

F32 = jnp.float32
BF16 = jnp.bfloat16

D_MODEL = 4096
BATCH = 8
SEQ = 2048
CTX_LEN = 256
GROUP = 16
STATE = 64
GRID_W = 64
DEPTH = 2
LN_EPS = 1e-5
ALPHA = (2.0 * DEPTH) ** 0.25
MOD_ROWS = 16
N_MOD = 6

LANES = 128
SUBLANES = 8
VMEM_CEILING_BYTES = 58 * 1024 * 1024

T_CHUNK = 16
N_SLAB = D_MODEL // LANES
G_SLAB = LANES // GROUP
SP = G_SLAB * STATE
J_LAT = SEQ // T_CHUNK
J_CTX = CTX_LEN // T_CHUNK
JB = 64
N_JB = J_LAT // JB
KT = T_CHUNK * LANES
POW_ROWS = 32


def _cparams(sem, nbytes):
    limit = int(min(VMEM_CEILING_BYTES, max(32 * 1024 * 1024, nbytes)))
    return pltpu.CompilerParams(dimension_semantics=sem, vmem_limit_bytes=limit)


def _mod_spec(k, grid_rank):
    if grid_rank == 1:
        return pl.BlockSpec((None, MOD_ROWS, D_MODEL), lambda i: (k, 0, 0))
    return pl.BlockSpec((None, MOD_ROWS, D_MODEL), lambda i, j: (k, 0, 0))


def _mod_kernel(c_ref, w_ref, b_ref, o_ref):
    c = c_ref[...]
    s = (c * jax.nn.sigmoid(c)).astype(BF16)
    o_ref[...] = jnp.dot(s, w_ref[...].astype(BF16), preferred_element_type=F32) + b_ref[...]


def _mod_rows(c_rows, w_mod, b_mod, layer):
    m, k = c_rows.shape
    n = w_mod.shape[2]
    tn = 512
    per = D_MODEL // tn
    return pl.pallas_call(
        _mod_kernel,
        grid=(n // tn,),
        in_specs=[pl.BlockSpec((m, k), lambda j: (0, 0)),
                  pl.BlockSpec((None, k, tn), lambda j: (layer, 0, j)),
                  pl.BlockSpec((None, 1, tn), lambda j: (layer, 0, j))],
        out_specs=pl.BlockSpec((None, m, tn), lambda j: (j // per, 0, j % per)),
        out_shape=jax.ShapeDtypeStruct((N_MOD, m, D_MODEL), F32),
        compiler_params=_cparams(("arbitrary",), 3 * k * tn * 4),
        name="mod_rows",
    )(c_rows, w_mod, b_mod.reshape(DEPTH, 1, n))


def _modulate_kernel(x_ref, sh_ref, sc_ref, o_ref, *, per, fixed_row):
    row = fixed_row if fixed_row is not None else pl.program_id(0) // per
    sh = sh_ref[pl.ds(row, 1), :]
    sc = sc_ref[pl.ds(row, 1), :]
    o_ref[...] = (x_ref[...] * (1.0 + sc) + sh).astype(o_ref.dtype)


def _modulate(x, mod, k_shift, k_scale, fixed_row=None, tr=512):
    b, n, d = x.shape
    tr = min(tr, n)
    per = n // tr
    return pl.pallas_call(
        functools.partial(_modulate_kernel, per=per, fixed_row=fixed_row),
        grid=(b * per,),
        in_specs=[pl.BlockSpec((None, tr, d), lambda i: (i // per, i % per, 0)),
                  _mod_spec(k_shift, 1), _mod_spec(k_scale, 1)],
        out_specs=pl.BlockSpec((tr, d), lambda i: (i, 0)),
        out_shape=jax.ShapeDtypeStruct((b * n, d), BF16),
        compiler_params=_cparams(("arbitrary",), 2 * tr * d * 6 + (8 << 20)),
        name="modulate",
    )(x, mod, mod)


def _resid_ln_kernel(*refs, with_h, per):
    if with_h:
        xl_ref, o_ref, g_ref, lg_ref, lb_ref, sh_ref, sc_ref, xo_ref, ho_ref = refs
    else:
        xl_ref, o_ref, g_ref, lg_ref, lb_ref, xo_ref = refs
    b = pl.program_id(0) // per
    y = ALPHA * xl_ref[...] + g_ref[pl.ds(b, 1), :] * o_ref[...]
    mu = jnp.mean(y, axis=-1, keepdims=True)
    yc = y - mu
    var = jnp.mean(yc * yc, axis=-1, keepdims=True)
    xn = yc * lax.rsqrt(var + LN_EPS) * lg_ref[...] + lb_ref[...]
    xo_ref[...] = xn
    if with_h:
        ho_ref[...] = (xn * (1.0 + sc_ref[pl.ds(b, 1), :]) + sh_ref[pl.ds(b, 1), :]).astype(ho_ref.dtype)


def _resid_ln(xl, o, gate_mod, k_gate, ln_g, ln_b, layer, sub, next_mod=None, k_shift=None, k_scale=None,
              tr=256):
    m, d = xl.shape
    per = SEQ // tr
    with_h = next_mod is not None
    row = pl.BlockSpec((tr, d), lambda i: (i, 0))
    vec = pl.BlockSpec((None, None, 1, d), lambda i: (layer, sub, 0, 0))
    in_specs = [row, row, _mod_spec(k_gate, 1), vec, vec]
    args = [xl, o, gate_mod, ln_g.reshape(DEPTH, 2, 1, d), ln_b.reshape(DEPTH, 2, 1, d)]
    out_specs = [row]
    out_shape = [jax.ShapeDtypeStruct((m, d), F32)]
    if with_h:
        in_specs += [_mod_spec(k_shift, 1), _mod_spec(k_scale, 1)]
        args += [next_mod, next_mod]
        out_specs.append(row)
        out_shape.append(jax.ShapeDtypeStruct((m, d), BF16))
    res = pl.pallas_call(
        functools.partial(_resid_ln_kernel, with_h=with_h, per=per),
        grid=(m // tr,),
        in_specs=in_specs, out_specs=out_specs, out_shape=out_shape,
        compiler_params=_cparams(("arbitrary",), 2 * tr * d * 14 + (12 << 20)),
        name="resid_ln",
    )(*args)
    return res if with_h else res[0]


def _mm_kernel(*refs, n_w, epilogue, chunked_lhs, out_batches, tn):
    lhs_ref = refs[0]
    w_refs = refs[1:1 + n_w]
    n_extra = 1 if epilogue == "conv" else 0
    extra = refs[1 + n_w:1 + n_w + n_extra]
    o_ref = refs[1 + n_w + n_extra]
    scratch = refs[2 + n_w + n_extra:]
    j = pl.program_id(1)
    if chunked_lhs:
        lhs_b_ref, = scratch
        tm, k = lhs_b_ref.shape

        @pl.when(j == 0)
        def _cast():
            lhs_b_ref[...] = lhs_ref[...].reshape(tm, k).astype(BF16)

        lhs_b = lhs_b_ref[...]
    else:
        lhs_b = lhs_ref[...]
    accs = [jnp.dot(lhs_b, w[...].astype(BF16), preferred_element_type=F32) for w in w_refs]
    if epilogue == "plain":
        out = accs[0]
    elif epilogue == "glu":
        zt = lhs_ref[:, :, pl.ds(pl.multiple_of(j * tn, tn), tn)]
        out = zt.reshape(accs[0].shape) * jax.nn.sigmoid(accs[0])
    elif epilogue == "swiglu":
        a1, a3 = accs
        out = a1 * jax.nn.sigmoid(a1) * a3
    elif epilogue == "conv":
        gb, gc, v = accs
        cw_ref, = extra
        t = gc * v
        tm = t.shape[0]
        pos = lax.broadcasted_iota(jnp.int32, t.shape, 0) % GRID_W
        prev = jnp.where(pos == 0, 0.0, pltpu.roll(t, 1, 0))
        nxt = jnp.where(pos == GRID_W - 1, 0.0, pltpu.roll(t, tm - 1, 0))
        out = gb * (cw_ref[0:1, :] * prev + cw_ref[1:2, :] * t + cw_ref[2:3, :] * nxt)
    else:
        raise ValueError(epilogue)
    out = out.astype(o_ref.dtype)
    if out_batches is None:
        o_ref[...] = out
    elif out_batches == 1:
        o_ref[...] = out.reshape(o_ref.shape)
    else:
        n_tok = out.shape[0] // out_batches
        for bb in range(out_batches):
            o_ref[:, bb, :, :] = out[bb * n_tok:(bb + 1) * n_tok].reshape(n_tok // T_CHUNK, T_CHUNK, tn)


def _matmul(lhs, weights, n_out, *, tm, tn, epilogue, out_dtype, chunked_lhs=False, chunk_tokens=None,
            extra=(), extra_specs=(), single_buffer_lhs=False, name="matmul"):
    k = weights[0][0].shape[1]
    buffered = dict(pipeline_mode=pl.Buffered(1)) if single_buffer_lhs else {}
    scratch_shapes = []
    if chunked_lhs:
        jn, bn, tt, _ = lhs.shape
        m = jn * bn * tt
        per = (jn * tt) // tm
        lhs_spec = pl.BlockSpec((tm // tt, None, tt, k), lambda i, j: (i % per, i // per, 0, 0), **buffered)
        scratch_shapes.append(pltpu.VMEM((tm, k), BF16))
    else:
        m = lhs.shape[0]
        lhs_spec = pl.BlockSpec((tm, k), lambda i, j: (i, 0), **buffered)
    w_specs = [pl.BlockSpec((None, k, tn), functools.partial(lambda i, j, layer, blk: (layer, 0, j + blk),
                                                             layer=layer, blk=off // tn))
               for (_, layer, off) in weights]
    out_batches = None
    if chunk_tokens is not None:
        jn = chunk_tokens // T_CHUNK
        if tm <= chunk_tokens:
            out_batches = 1
            per_o = chunk_tokens // tm
            out_spec = pl.BlockSpec((tm // T_CHUNK, None, T_CHUNK, tn), lambda i, j: (i % per_o, i // per_o, 0, j))
        else:
            out_batches = tm // chunk_tokens
            out_spec = pl.BlockSpec((jn, out_batches, T_CHUNK, tn), lambda i, j: (0, i, 0, j))
        out_shape = jax.ShapeDtypeStruct((jn, BATCH, T_CHUNK, n_out), out_dtype)
    else:
        out_spec = pl.BlockSpec((tm, tn), lambda i, j: (i, j))
        out_shape = jax.ShapeDtypeStruct((m, n_out), out_dtype)
    w_arrays = [w for (w, _, _) in weights]
    lhs_bytes = tm * k * lhs.dtype.itemsize * (1 if single_buffer_lhs else 2) + (tm * k * 2 if chunked_lhs else 0)
    w_bytes = sum(2 * k * tn * w.dtype.itemsize + (k * tn * 2 if w.dtype != BF16 else 0) for w in w_arrays)
    out_bytes = tm * tn * (2 * jnp.dtype(out_dtype).itemsize + 4 * (len(weights) + 3))
    return pl.pallas_call(
        functools.partial(_mm_kernel, n_w=len(weights), epilogue=epilogue, chunked_lhs=chunked_lhs,
                          out_batches=out_batches, tn=tn),
        grid=(m // tm, n_out // tn),
        in_specs=[lhs_spec] + w_specs + list(extra_specs),
        out_specs=out_spec, out_shape=out_shape,
        scratch_shapes=scratch_shapes,
        compiler_params=_cparams(("arbitrary", "arbitrary"), lhs_bytes + w_bytes + out_bytes + (4 << 20)),
        name=name,
    )(lhs, *w_arrays, *extra)


def _s5_prep_kernel(lr_ref, li_ref, ls_ref, bre_ref, bim_ref, cre_ref, cim_ref,
                    tall_ref, w_ref, v_ref, at_ref, *, reverse):
    lr = lr_ref[...]
    li = li_ref[...]
    dt = jnp.exp(ls_ref[...])
    kk = jnp.minimum(lax.broadcasted_iota(jnp.int32, (POW_ROWS, SP), 0), T_CHUNK).astype(F32)
    mag = jnp.exp(kk * (lr * dt))
    ang = kk * (li * dt)
    p_re = mag * jnp.cos(ang)
    p_im = mag * jnp.sin(ang)
    a_re = p_re[1:2, :]
    a_im = p_im[1:2, :]
    n_re, n_im = a_re - 1.0, a_im
    den = lr * lr + li * li
    f_re = (n_re * lr + n_im * li) / den
    f_im = (n_im * lr - n_re * li) / den
    b_re = bre_ref[...]
    b_im = bim_ref[...]
    bb_re = f_re * b_re - f_im * b_im
    bb_im = f_re * b_im + f_im * b_re
    c_re = cre_ref[...]
    c_im = cim_ref[...]
    c_re_b = c_re.astype(BF16)
    c_im_b = c_im.astype(BF16)

    dks = []
    for k in range(T_CHUNK):
        pr, pi = p_re[k:k + 1, :], p_im[k:k + 1, :]
        bk_re = (bb_re * pr - bb_im * pi).astype(BF16)
        bk_im = (bb_re * pi + bb_im * pr).astype(BF16)
        dk = (jnp.dot(bk_re, c_re_b, preferred_element_type=F32)
              - jnp.dot(bk_im, c_im_b, preferred_element_type=F32))
        dks.append(dk.astype(BF16))
        m = k if reverse else T_CHUNK - 1 - k
        w_ref[m * LANES:(m + 1) * LANES, 0:SP] = bk_re
        w_ref[m * LANES:(m + 1) * LANES, SP:2 * SP] = bk_im

    zero = jnp.zeros((LANES, LANES), BF16)
    for r in range(T_CHUNK):
        for c in range(2):
            idx = (r - c) if reverse else (T_CHUNK - 2 - r + c)
            blk = dks[idx] if 0 <= idx < T_CHUNK else zero
            tall_ref[r * LANES:(r + 1) * LANES, c * LANES:(c + 1) * LANES] = blk

    pad = jnp.zeros((LANES - POW_ROWS, SP), F32)
    pt_re = jnp.concatenate([p_re, pad], axis=0).T
    pt_im = jnp.concatenate([p_im, pad], axis=0).T
    for t in range(T_CHUNK):
        k = (T_CHUNK - t) if reverse else (t + 1)
        pr, pi = pt_re[:, k:k + 1], pt_im[:, k:k + 1]
        v_ref[0:SP, t * LANES:(t + 1) * LANES] = (pr * c_re - pi * c_im).astype(BF16)
        v_ref[SP:2 * SP, t * LANES:(t + 1) * LANES] = (-(pr * c_im + pi * c_re)).astype(BF16)

    at_ref[:, 0:SP] = p_re[T_CHUNK:T_CHUNK + 1, :]
    at_ref[:, SP:2 * SP] = p_im[T_CHUNK:T_CHUNK + 1, :]


def _s5_prep(lam_re, lam_im, log_step, b_re, b_im, c_re, c_im, *, reverse):
    def rows(v):
        return v.reshape(N_SLAB, 1, SP)

    eye = jnp.eye(G_SLAB, dtype=bool)

    def b_blockdiag(b):
        bt = b.reshape(N_SLAB, G_SLAB, STATE, GROUP).transpose(0, 1, 3, 2)
        bd = jnp.where(eye[None, :, None, :, None], bt[:, :, :, None, :], 0.0)
        return bd.reshape(N_SLAB, LANES, SP)

    def c_blockdiag(c):
        ct = c.reshape(N_SLAB, G_SLAB, GROUP, STATE).transpose(0, 1, 3, 2)
        cd = jnp.where(eye[None, :, None, :, None], ct[:, :, :, None, :], 0.0)
        return cd.reshape(N_SLAB, SP, LANES)

    ls = jnp.repeat(log_step, STATE).reshape(N_SLAB, 1, SP)
    vec = pl.BlockSpec((None, 1, SP), lambda s: (s, 0, 0))
    bsp = pl.BlockSpec((None, LANES, SP), lambda s: (s, 0, 0))
    csp = pl.BlockSpec((None, SP, LANES), lambda s: (s, 0, 0))
    return pl.pallas_call(
        functools.partial(_s5_prep_kernel, reverse=reverse),
        grid=(N_SLAB,),
        in_specs=[vec, vec, vec, bsp, bsp, csp, csp],
        out_specs=[pl.BlockSpec((None, KT, 2 * LANES), lambda s: (s, 0, 0)),
                   pl.BlockSpec((None, KT, 2 * SP), lambda s: (s, 0, 0)),
                   pl.BlockSpec((None, 2 * SP, KT), lambda s: (s, 0, 0)),
                   pl.BlockSpec((None, 1, 2 * SP), lambda s: (s, 0, 0))],
        out_shape=[jax.ShapeDtypeStruct((N_SLAB, KT, 2 * LANES), BF16),
                   jax.ShapeDtypeStruct((N_SLAB, KT, 2 * SP), BF16),
                   jax.ShapeDtypeStruct((N_SLAB, 2 * SP, KT), BF16),
                   jax.ShapeDtypeStruct((N_SLAB, 1, 2 * SP), F32)],
        compiler_params=_cparams(("arbitrary",), 40 << 20),
        name="s5_prep_bwd" if reverse else "s5_prep_fwd",
    )(rows(lam_re), rows(lam_im), ls, b_blockdiag(b_re), b_blockdiag(b_im),
      c_blockdiag(c_re), c_blockdiag(c_im))


def _s5_scan_kernel(*refs, reverse, second):
    it = iter(refs)
    ul_ref = next(it)
    uc_ref = next(it)
    y1_ref = next(it) if second else None
    tall_ref, w_ref, v_ref, at_ref = next(it), next(it), next(it), next(it)
    dsk_ref = None if second else next(it)
    o_ref = next(it)
    x2_ref, xc2_ref, xst_ref, sprev_ref, state_ref = it

    rows = JB * BATCH
    crow = J_CTX * BATCH
    a_re = at_ref[:, 0:SP]
    a_im = at_ref[:, SP:2 * SP]

    def token(ref, t, n):
        return ref[pl.ds(t, n, stride=T_CHUNK), :]

    def scan(n_chunks, keep):
        s_re = state_ref[:, 0:SP]
        s_im = state_ref[:, SP:2 * SP]
        order = range(n_chunks - 1, -1, -1) if reverse else range(n_chunks)
        for jj in order:
            r = slice(jj * BATCH, (jj + 1) * BATCH)
            if keep:
                sprev_ref[r, 0:SP] = s_re
                sprev_ref[r, SP:2 * SP] = s_im
            x_re = xst_ref[r, 0:SP]
            x_im = xst_ref[r, SP:2 * SP]
            s_re, s_im = (a_re * s_re - a_im * s_im + x_re,
                          a_re * s_im + a_im * s_re + x_im)
        state_ref[:, 0:SP] = s_re
        state_ref[:, SP:2 * SP] = s_im

    @pl.when(pl.program_id(1) == 0)
    def _context():
        state_ref[...] = jnp.zeros_like(state_ref)
        for t in range(T_CHUNK):
            xc2_ref[:, t * LANES:(t + 1) * LANES] = token(uc_ref, t, crow).astype(BF16)
        xst_ref[0:crow, :] = jnp.dot(xc2_ref[...], w_ref[...], preferred_element_type=F32)
        scan(J_CTX, keep=False)

    for t in range(T_CHUNK):
        x2_ref[:, t * LANES:(t + 1) * LANES] = token(ul_ref, t, rows).astype(BF16)
    xst_ref[...] = jnp.dot(x2_ref[...], w_ref[...], preferred_element_type=F32)
    scan(JB, keep=True)
    sp = sprev_ref[...].astype(BF16)

    n_tiles = KT // (2 * LANES)
    for nt in range(n_tiles):
        cols = slice(nt * 2 * LANES, (nt + 1) * 2 * LANES)
        if reverse:
            xa = x2_ref[:, nt * 2 * LANES:]
            ta = tall_ref[0:(n_tiles - nt) * 2 * LANES, :]
        else:
            xa = x2_ref[:, 0:(nt + 1) * 2 * LANES]
            ta = tall_ref[(n_tiles - 1 - nt) * 2 * LANES:, :]
        y = (jnp.dot(xa, ta, preferred_element_type=F32)
             + jnp.dot(sp, v_ref[:, cols], preferred_element_type=F32))
        for c in range(2):
            t = 2 * nt + c
            yt = y[:, c * LANES:(c + 1) * LANES]
            if second:
                val = jax.nn.gelu(token(y1_ref, t, rows) + yt)
            else:
                val = dsk_ref[...] * token(ul_ref, t, rows) + yt
            o_ref[pl.ds(t, rows, stride=T_CHUNK), :] = val


def _s5_scan(u_lat, u_ctx, ops, *, reverse, y_prev=None, d_skip=None):
    tall, w, v, at = ops
    second = y_prev is not None
    blk_rows = JB * BATCH * T_CHUNK
    ctx_rows = J_CTX * BATCH * T_CHUNK

    def jblk(q):
        return (N_JB - 1 - q) if reverse else q

    lat_spec = pl.BlockSpec((blk_rows, LANES), lambda s, q: (jblk(q), s))
    in_specs = [lat_spec, pl.BlockSpec((ctx_rows, LANES), lambda s, q: (0, s))]
    args = [u_lat.reshape(J_LAT * BATCH * T_CHUNK, D_MODEL), u_ctx.reshape(ctx_rows, D_MODEL)]
    if second:
        in_specs.append(lat_spec)
        args.append(y_prev.reshape(J_LAT * BATCH * T_CHUNK, D_MODEL))
    in_specs += [pl.BlockSpec((None, KT, 2 * LANES), lambda s, q: (s, 0, 0)),
                 pl.BlockSpec((None, KT, 2 * SP), lambda s, q: (s, 0, 0)),
                 pl.BlockSpec((None, 2 * SP, KT), lambda s, q: (s, 0, 0)),
                 pl.BlockSpec((None, 1, 2 * SP), lambda s, q: (s, 0, 0))]
    args += [tall, w, v, at]
    if not second:
        in_specs.append(pl.BlockSpec((None, None, 1, LANES), lambda s, q: (0, s, 0, 0)))
        args.append(d_skip.reshape(1, N_SLAB, 1, LANES))
    rows = JB * BATCH
    out = pl.pallas_call(
        functools.partial(_s5_scan_kernel, reverse=reverse, second=second),
        grid=(N_SLAB, N_JB),
        in_specs=in_specs,
        out_specs=lat_spec,
        out_shape=jax.ShapeDtypeStruct((J_LAT * BATCH * T_CHUNK, D_MODEL), F32),
        scratch_shapes=[pltpu.VMEM((rows, KT), BF16),
                        pltpu.VMEM((J_CTX * BATCH, KT), BF16),
                        pltpu.VMEM((rows, 2 * SP), F32),
                        pltpu.VMEM((rows, 2 * SP), F32),
                        pltpu.VMEM((BATCH, 2 * SP), F32)],
        compiler_params=_cparams(("arbitrary", "arbitrary"), VMEM_CEILING_BYTES),
        name="s5_scan_bwd" if reverse else "s5_scan_fwd",
    )(*args)
    return out.reshape(J_LAT, BATCH, T_CHUNK, D_MODEL)


def kernel(x, c, ctx, c_ctx, w_mod, b_mod, ln_g, ln_b, w_in_a, lam_re, lam_im, log_step,
           b_re, b_im, c_re, c_im, d_skip, w_glu_a, w_out_a, w_in_b, conv_w, w_out_b, w1, w3, w2):
    d = D_MODEL
    d_ff = w1.shape[-1]
    m = BATCH * SEQ

    c_rows = jnp.concatenate([c, c_ctx[None, :], jnp.zeros((MOD_ROWS - BATCH - 1, d), F32)], axis=0)
    mods = [_mod_rows(c_rows, w_mod, b_mod, i) for i in range(DEPTH)]

    def ffn(h, xl, layer, next_layer):
        g = _matmul(h, [(w1, layer, 0), (w3, layer, 0)], d_ff, tm=2048, tn=256, epilogue="swiglu",
                    out_dtype=BF16, single_buffer_lhs=True, name="mm_swiglu")
        o = _matmul(g, [(w2, layer, 0)], d, tm=1024, tn=256, epilogue="plain", out_dtype=F32,
                    single_buffer_lhs=True, name="mm_w2")
        if next_layer is None:
            return _resid_ln(xl, o, mods[layer], 5, ln_g, ln_b, layer, 1)
        return _resid_ln(xl, o, mods[layer], 5, ln_g, ln_b, layer, 1, mods[next_layer], 0, 1)

    h_l = _modulate(x, mods[0], 0, 1)
    h_c = _modulate(ctx, mods[0], 0, 1, fixed_row=BATCH)
    u_l = _matmul(h_l, [(w_in_a, 0, 0)], d, tm=1024, tn=512, epilogue="plain", out_dtype=F32,
                  chunk_tokens=SEQ, name="mm_in_a")
    u_c = _matmul(h_c, [(w_in_a, 0, 0)], d, tm=BATCH * CTX_LEN, tn=512, epilogue="plain", out_dtype=F32,
                  chunk_tokens=CTX_LEN, single_buffer_lhs=True, name="mm_in_a_ctx")
    ops_f = _s5_prep(lam_re[0, 0], lam_im[0, 0], log_step[0, 0], b_re[0, 0], b_im[0, 0],
                     c_re[0, 0], c_im[0, 0], reverse=False)
    ops_b = _s5_prep(lam_re[0, 1], lam_im[0, 1], log_step[0, 1], b_re[0, 1], b_im[0, 1],
                     c_re[0, 1], c_im[0, 1], reverse=True)
    y_f = _s5_scan(u_l, u_c, ops_f, reverse=False, d_skip=d_skip)
    z = _s5_scan(u_l, u_c, ops_b, reverse=True, y_prev=y_f)
    gz = _matmul(z, [(w_glu_a, 0, 0)], d, tm=1024, tn=512, epilogue="glu", out_dtype=BF16,
                 chunked_lhs=True, single_buffer_lhs=True, name="mm_glu")
    o = _matmul(gz, [(w_out_a, 0, 0)], d, tm=1024, tn=512, epilogue="plain", out_dtype=F32, name="mm_out_a")
    xl, h = _resid_ln(x.reshape(m, d), o, mods[0], 2, ln_g, ln_b, 0, 0, mods[0], 3, 4)
    xl, h = ffn(h, xl, 0, 1)

    cw_spec = pl.BlockSpec((None, 3, 256), lambda i, j: (0, 0, j))
    gz = _matmul(h, [(w_in_b, 0, 0), (w_in_b, 0, d), (w_in_b, 0, 2 * d)], d, tm=1024, tn=256, epilogue="conv",
                 out_dtype=BF16, extra=(conv_w,), extra_specs=(cw_spec,), single_buffer_lhs=True, name="mm_conv")
    o = _matmul(gz, [(w_out_b, 0, 0)], d, tm=1024, tn=512, epilogue="plain", out_dtype=F32, name="mm_out_b")
    xl, h = _resid_ln(xl, o, mods[1], 2, ln_g, ln_b, 1, 0, mods[1], 3, 4)
    xl = ffn(h, xl, 1, None)
    return xl.reshape(BATCH, SEQ, d)
```

```python
import functools

import jax
import jax.numpy as jnp
from jax import lax
from jax.experimental import pallas as pl
from jax.experimental.pallas import tpu as pltpu

F32 = jnp.float32
BF16 = jnp.bfloat16

D_MODEL = 4096
BATCH = 8
SEQ = 2048
CTX_LEN = 256
GROUP = 16
STATE = 64
GRID_W = 64
DEPTH = 2
LN_EPS = 1e-5
ALPHA = (2.0 * DEPTH) ** 0.25
MOD_ROWS = 16
N_MOD = 6

LANES = 128
SUBLANES = 8
VMEM_CEILING_BYTES = 60 * 1024 * 1024

T_CHUNK = 16
N_SLAB = D_MODEL // LANES
G_SLAB = LANES // GROUP
SP = G_SLAB * STATE
J_LAT = SEQ // T_CHUNK
J_CTX = CTX_LEN // T_CHUNK
JB = 64
N_JB = J_LAT // JB
KT = T_CHUNK * LANES
POW_ROWS = 32
NORM_ROWS = 16
NORM_UNROLL = 8


def _cparams(sem, nbytes):
    limit = int(min(VMEM_CEILING_BYTES, max(32 * 1024 * 1024, nbytes)))
    return pltpu.CompilerParams(dimension_semantics=sem, vmem_limit_bytes=limit)


def _mod_spec(k, grid_rank):
    if grid_rank == 1:
        return pl.BlockSpec((None, MOD_ROWS, D_MODEL), lambda i: (k, 0, 0))
    return pl.BlockSpec((None, MOD_ROWS, D_MODEL), lambda i, j: (k, 0, 0))


def _mod_kernel(c_ref, w_ref, b_ref, o_ref):
    c = c_ref[...]
    s = (c * jax.nn.sigmoid(c)).astype(BF16)
    o_ref[...] = jnp.dot(s, w_ref[...].astype(BF16), preferred_element_type=F32) + b_ref[...]


def _mod_rows(c_rows, w_mod, b_mod, layer):
    m, k = c_rows.shape
    n = w_mod.shape[2]
    tn = 512
    per = D_MODEL // tn
    return pl.pallas_call(
        _mod_kernel,
        grid=(n // tn,),
        in_specs=[pl.BlockSpec((m, k), lambda j: (0, 0)),
                  pl.BlockSpec((None, k, tn), lambda j: (layer, 0, j)),
                  pl.BlockSpec((None, 1, tn), lambda j: (layer, 0, j))],
        out_specs=pl.BlockSpec((None, m, tn), lambda j: (j // per, 0, j % per)),
        out_shape=jax.ShapeDtypeStruct((N_MOD, m, D_MODEL), F32),
        compiler_params=_cparams(("arbitrary",), 3 * k * tn * 4),
        name="mod_rows",
    )(c_rows, w_mod, b_mod.reshape(DEPTH, 1, n))


def _modulate_kernel(x_ref, sh_ref, sc_ref, o_ref, *, per, fixed_row):
    row = fixed_row if fixed_row is not None else pl.program_id(0) // per
    sh = sh_ref[pl.ds(row, 1), :]
    sc = sc_ref[pl.ds(row, 1), :]
    o_ref[...] = (x_ref[...] * (1.0 + sc) + sh).astype(o_ref.dtype)


def _modulate(x, mod, k_shift, k_scale, fixed_row=None, tr=512):
    b, n, d = x.shape
    tr = min(tr, n)
    per = n // tr
    return pl.pallas_call(
        functools.partial(_modulate_kernel, per=per, fixed_row=fixed_row),
        grid=(b * per,),
        in_specs=[pl.BlockSpec((None, tr, d), lambda i: (i // per, i % per, 0)),
                  _mod_spec(k_shift, 1), _mod_spec(k_scale, 1)],
        out_specs=pl.BlockSpec((tr, d), lambda i: (i, 0)),
        out_shape=jax.ShapeDtypeStruct((b * n, d), BF16),
        compiler_params=_cparams(("arbitrary",), 2 * tr * d * 6 + (8 << 20)),
        name="modulate",
    )(x, mod, mod)


def _norm_kernel(*refs, with_mod, per):
    if with_mod:
        y_ref, lg_ref, lb_ref, sh_ref, sc_ref, o_ref, mu_ref, rs_ref = refs
    else:
        y_ref, lg_ref, lb_ref, o_ref = refs
    lg = lg_ref[...]
    lb = lb_ref[...]
    if with_mod:
        b = pl.program_id(0) // per
        sc1 = 1.0 + sc_ref[pl.ds(b, 1), :]
        sh = sh_ref[pl.ds(b, 1), :]

    def rows(c, carry):
        r = pl.ds(pl.multiple_of(c * NORM_ROWS, NORM_ROWS), NORM_ROWS)
        y = y_ref[r, :]
        mu = jnp.mean(y, axis=-1, keepdims=True)
        yc = y - mu
        rs = lax.rsqrt(jnp.mean(yc * yc, axis=-1, keepdims=True) + LN_EPS)
        xn = yc * rs * lg + lb
        if with_mod:
            xn = xn * sc1 + sh
            mu_ref[r, :] = jnp.broadcast_to(mu, (NORM_ROWS, LANES))
            rs_ref[r, :] = jnp.broadcast_to(rs, (NORM_ROWS, LANES))
        o_ref[r, :] = xn.astype(o_ref.dtype)
        return carry

    lax.fori_loop(0, y_ref.shape[0] // NORM_ROWS, rows, 0, unroll=NORM_UNROLL)


def _ln_vec_spec(layer, sub, width, grid_rank):
    if grid_rank == 1:
        return pl.BlockSpec((None, None, 1, width), lambda i: (layer, sub, 0, 0))
    return pl.BlockSpec((None, None, 1, width), lambda i, j: (layer, sub, 0, j))


def _norm(y, ln_g, ln_b, layer, sub, mod=None, k_shift=None, k_scale=None, tr=512):
    m, d = y.shape
    per = SEQ // tr
    with_mod = mod is not None
    row = pl.BlockSpec((tr, d), lambda i: (i, 0))
    stat = pl.BlockSpec((tr, LANES), lambda i: (i, 0))
    vec = _ln_vec_spec(layer, sub, d, 1)
    in_specs = [row, vec, vec]
    args = [y, ln_g.reshape(DEPTH, 2, 1, d), ln_b.reshape(DEPTH, 2, 1, d)]
    if with_mod:
        in_specs += [_mod_spec(k_shift, 1), _mod_spec(k_scale, 1)]
        args += [mod, mod]
        out_specs = [row, stat, stat]
        out_shape = [jax.ShapeDtypeStruct((m, d), BF16), jax.ShapeDtypeStruct((m, LANES), F32),
                     jax.ShapeDtypeStruct((m, LANES), F32)]
    else:
        out_specs = row
        out_shape = jax.ShapeDtypeStruct((m, d), F32)
    return pl.pallas_call(
        functools.partial(_norm_kernel, with_mod=with_mod, per=per),
        grid=(m // tr,),
        in_specs=in_specs, out_specs=out_specs, out_shape=out_shape,
        compiler_params=_cparams(("arbitrary",), 2 * tr * d * 12 + (8 << 20)),
        name="norm_mod" if with_mod else "norm_out",
    )(*args)


def _mm_resid_kernel(*refs, has_stats, per):
    if has_stats:
        lhs_ref, w_ref, prev_ref, mu_ref, rs_ref, lg_ref, lb_ref, g_ref, y_ref = refs
    else:
        lhs_ref, w_ref, prev_ref, g_ref, y_ref = refs
    acc = jnp.dot(lhs_ref[...], w_ref[...].astype(BF16), preferred_element_type=F32)
    prev = prev_ref[...]
    if has_stats:
        prev = (prev - mu_ref[:, 0:1]) * rs_ref[:, 0:1] * lg_ref[...] + lb_ref[...]
    y_ref[...] = ALPHA * prev + g_ref[pl.ds(pl.program_id(0) // per, 1), :] * acc


def _matmul_resid(lhs, w, layer, prev, gate_mod, k_gate, *, tm, tn, prev_ln=None, ln_g=None, ln_b=None,
                  name="mm_resid"):
    m, k = lhs.shape
    d = D_MODEL
    per = SEQ // tm
    has_stats = prev_ln is not None
    tile = pl.BlockSpec((tm, tn), lambda i, j: (i, j))
    in_specs = [pl.BlockSpec((tm, k), lambda i, j: (i, 0), pipeline_mode=pl.Buffered(1)),
                pl.BlockSpec((None, k, tn), lambda i, j: (layer, 0, j)), tile]
    if has_stats:
        y_prev, mu_prev, rs_prev = prev
        vec = _ln_vec_spec(prev_ln[0], prev_ln[1], tn, 2)
        stat_in = pl.BlockSpec((tm, LANES), lambda i, j: (i, 0), pipeline_mode=pl.Buffered(1))
        in_specs += [stat_in, stat_in, vec, vec]
        args = [lhs, w, y_prev, mu_prev, rs_prev, ln_g.reshape(DEPTH, 2, 1, d), ln_b.reshape(DEPTH, 2, 1, d)]
    else:
        args = [lhs, w, prev]
    in_specs.append(pl.BlockSpec((None, MOD_ROWS, tn), lambda i, j: (k_gate, 0, j)))
    args.append(gate_mod)
    nbytes = (tm * k * lhs.dtype.itemsize + k * tn * (2 * w.dtype.itemsize + 2) + tm * tn * 4 * 8
              + tm * LANES * 4 * 2 + (4 << 20))
    return pl.pallas_call(
        functools.partial(_mm_resid_kernel, has_stats=has_stats, per=per),
        grid=(m // tm, d // tn),
        in_specs=in_specs,
        out_specs=tile,
        out_shape=jax.ShapeDtypeStruct((m, d), F32),
        compiler_params=_cparams(("arbitrary", "arbitrary"), nbytes),
        name=name,
    )(*args)


def _mm_kernel(*refs, n_w, epilogue, chunked_lhs, out_batches, tn):
    lhs_ref = refs[0]
    w_refs = refs[1:1 + n_w]
    n_extra = 1 if epilogue == "conv" else 0
    extra = refs[1 + n_w:1 + n_w + n_extra]
    o_ref = refs[1 + n_w + n_extra]
    scratch = refs[2 + n_w + n_extra:]
    j = pl.program_id(1)
    if chunked_lhs:
        lhs_b_ref, = scratch
        tm, k = lhs_b_ref.shape

        @pl.when(j == 0)
        def _cast():
            lhs_b_ref[...] = lhs_ref[...].reshape(tm, k).astype(BF16)

        lhs_b = lhs_b_ref[...]
    else:
        lhs_b = lhs_ref[...]
    accs = [jnp.dot(lhs_b, w[...].astype(BF16), preferred_element_type=F32) for w in w_refs]
    if epilogue == "plain":
        out = accs[0]
    elif epilogue == "glu":
        zt = lhs_ref[:, :, pl.ds(pl.multiple_of(j * tn, tn), tn)]
        out = zt.reshape(accs[0].shape) * jax.nn.sigmoid(accs[0])
    elif epilogue == "swiglu":
        a1, a3 = accs
        out = a1 * jax.nn.sigmoid(a1) * a3
    elif epilogue == "conv":
        gb, gc, v = accs
        cw_ref, = extra
        t = gc * v
        tm = t.shape[0]
        pos = lax.broadcasted_iota(jnp.int32, t.shape, 0) % GRID_W
        prev = jnp.where(pos == 0, 0.0, pltpu.roll(t, 1, 0))
        nxt = jnp.where(pos == GRID_W - 1, 0.0, pltpu.roll(t, tm - 1, 0))
        out = gb * (cw_ref[0:1, :] * prev + cw_ref[1:2, :] * t + cw_ref[2:3, :] * nxt)
    else:
        raise ValueError(epilogue)
    out = out.astype(o_ref.dtype)
    if out_batches is None:
        o_ref[...] = out
    elif out_batches == 1:
        o_ref[...] = out.reshape(o_ref.shape)
    else:
        n_tok = out.shape[0] // out_batches
        for bb in range(out_batches):
            o_ref[:, bb, :, :] = out[bb * n_tok:(bb + 1) * n_tok].reshape(n_tok // T_CHUNK, T_CHUNK, tn)


def _matmul(lhs, weights, n_out, *, tm, tn, epilogue, out_dtype, chunked_lhs=False, chunk_tokens=None,
            extra=(), extra_specs=(), single_buffer_lhs=False, name="matmul"):
    k = weights[0][0].shape[1]
    buffered = dict(pipeline_mode=pl.Buffered(1)) if single_buffer_lhs else {}
    scratch_shapes = []
    if chunked_lhs:
        jn, bn, tt, _ = lhs.shape
        m = jn * bn * tt
        per = (jn * tt) // tm
        lhs_spec = pl.BlockSpec((tm // tt, None, tt, k), lambda i, j: (i % per, i // per, 0, 0), **buffered)
        scratch_shapes.append(pltpu.VMEM((tm, k), BF16))
    else:
        m = lhs.shape[0]
        lhs_spec = pl.BlockSpec((tm, k), lambda i, j: (i, 0), **buffered)
    w_specs = [pl.BlockSpec((None, k, tn), functools.partial(lambda i, j, layer, blk: (layer, 0, j + blk),
                                                             layer=layer, blk=off // tn))
               for (_, layer, off) in weights]
    out_batches = None
    if chunk_tokens is not None:
        jn = chunk_tokens // T_CHUNK
        if tm <= chunk_tokens:
            out_batches = 1
            per_o = chunk_tokens // tm
            out_spec = pl.BlockSpec((tm // T_CHUNK, None, T_CHUNK, tn), lambda i, j: (i % per_o, i // per_o, 0, j))
        else:
            out_batches = tm // chunk_tokens
            out_spec = pl.BlockSpec((jn, out_batches, T_CHUNK, tn), lambda i, j: (0, i, 0, j))
        out_shape = jax.ShapeDtypeStruct((jn, BATCH, T_CHUNK, n_out), out_dtype)
    else:
        out_spec = pl.BlockSpec((tm, tn), lambda i, j: (i, j))
        out_shape = jax.ShapeDtypeStruct((m, n_out), out_dtype)
    w_arrays = [w for (w, _, _) in weights]
    lhs_bytes = tm * k * lhs.dtype.itemsize * (1 if single_buffer_lhs else 2) + (tm * k * 2 if chunked_lhs else 0)
    w_bytes = sum(2 * k * tn * w.dtype.itemsize + (k * tn * 2 if w.dtype != BF16 else 0) for w in w_arrays)
    out_bytes = tm * tn * (2 * jnp.dtype(out_dtype).itemsize + 4 * (len(weights) + 3))
    return pl.pallas_call(
        functools.partial(_mm_kernel, n_w=len(weights), epilogue=epilogue, chunked_lhs=chunked_lhs,
                          out_batches=out_batches, tn=tn),
        grid=(m // tm, n_out // tn),
        in_specs=[lhs_spec] + w_specs + list(extra_specs),
        out_specs=out_spec, out_shape=out_shape,
        scratch_shapes=scratch_shapes,
        compiler_params=_cparams(("arbitrary", "arbitrary"), lhs_bytes + w_bytes + out_bytes + (4 << 20)),
        name=name,
    )(lhs, *w_arrays, *extra)


def _s5_prep_kernel(lr_ref, li_ref, ls_ref, bre_ref, bim_ref, cre_ref, cim_ref,
                    tall_ref, w_ref, v_ref, at_ref, *, reverse):
    lr = lr_ref[...]
    li = li_ref[...]
    dt = jnp.exp(ls_ref[...])
    kk = jnp.minimum(lax.broadcasted_iota(jnp.int32, (POW_ROWS, SP), 0), T_CHUNK).astype(F32)
    mag = jnp.exp(kk * (lr * dt))
    ang = kk * (li * dt)
    p_re = mag * jnp.cos(ang)
    p_im = mag * jnp.sin(ang)
    a_re = p_re[1:2, :]
    a_im = p_im[1:2, :]
    n_re, n_im = a_re - 1.0, a_im
    den = lr * lr + li * li
    f_re = (n_re * lr + n_im * li) / den
    f_im = (n_im * lr - n_re * li) / den
    b_re = bre_ref[...]
    b_im = bim_ref[...]
    bb_re = f_re * b_re - f_im * b_im
    bb_im = f_re * b_im + f_im * b_re
    c_re = cre_ref[...]
    c_im = cim_ref[...]
    c_re_b = c_re.astype(BF16)
    c_im_b = c_im.astype(BF16)

    dks = []
    for k in range(T_CHUNK):
        pr, pi = p_re[k:k + 1, :], p_im[k:k + 1, :]
        bk_re = (bb_re * pr - bb_im * pi).astype(BF16)
        bk_im = (bb_re * pi + bb_im * pr).astype(BF16)
        dk = (jnp.dot(bk_re, c_re_b, preferred_element_type=F32)
              - jnp.dot(bk_im, c_im_b, preferred_element_type=F32))
        dks.append(dk.astype(BF16))
        m = k if reverse else T_CHUNK - 1 - k
        w_ref[m * LANES:(m + 1) * LANES, 0:SP] = bk_re
        w_ref[m * LANES:(m + 1) * LANES, SP:2 * SP] = bk_im

    zero = jnp.zeros((LANES, LANES), BF16)
    for r in range(T_CHUNK):
        for c in range(2):
            idx = (r - c) if reverse else (T_CHUNK - 2 - r + c)
            blk = dks[idx] if 0 <= idx < T_CHUNK else zero
            tall_ref[r * LANES:(r + 1) * LANES, c * LANES:(c + 1) * LANES] = blk

    pad = jnp.zeros((LANES - POW_ROWS, SP), F32)
    pt_re = jnp.concatenate([p_re, pad], axis=0).T
    pt_im = jnp.concatenate([p_im, pad], axis=0).T
    for t in range(T_CHUNK):
        k = (T_CHUNK - t) if reverse else (t + 1)
        pr, pi = pt_re[:, k:k + 1], pt_im[:, k:k + 1]
        v_ref[0:SP, t * LANES:(t + 1) * LANES] = (pr * c_re - pi * c_im).astype(BF16)
        v_ref[SP:2 * SP, t * LANES:(t + 1) * LANES] = (-(pr * c_im + pi * c_re)).astype(BF16)

    at_ref[:, 0:SP] = p_re[T_CHUNK:T_CHUNK + 1, :]
    at_ref[:, SP:2 * SP] = p_im[T_CHUNK:T_CHUNK + 1, :]


def _s5_prep(lam_re, lam_im, log_step, b_re, b_im, c_re, c_im, *, reverse):
    def rows(v):
        return v.reshape(N_SLAB, 1, SP)

    eye = jnp.eye(G_SLAB, dtype=bool)

    def b_blockdiag(b):
        bt = b.reshape(N_SLAB, G_SLAB, STATE, GROUP).transpose(0, 1, 3, 2)
        bd = jnp.where(eye[None, :, None, :, None], bt[:, :, :, None, :], 0.0)
        return bd.reshape(N_SLAB, LANES, SP)

    def c_blockdiag(c):
        ct = c.reshape(N_SLAB, G_SLAB, GROUP, STATE).transpose(0, 1, 3, 2)
        cd = jnp.where(eye[None, :, None, :, None], ct[:, :, :, None, :], 0.0)
        return cd.reshape(N_SLAB, SP, LANES)

    ls = jnp.repeat(log_step, STATE).reshape(N_SLAB, 1, SP)
    vec = pl.BlockSpec((None, 1, SP), lambda s: (s, 0, 0))
    bsp = pl.BlockSpec((None, LANES, SP), lambda s: (s, 0, 0))
    csp = pl.BlockSpec((None, SP, LANES), lambda s: (s, 0, 0))
    return pl.pallas_call(
        functools.partial(_s5_prep_kernel, reverse=reverse),
        grid=(N_SLAB,),
        in_specs=[vec, vec, vec, bsp, bsp, csp, csp],
        out_specs=[pl.BlockSpec((None, KT, 2 * LANES), lambda s: (s, 0, 0)),
                   pl.BlockSpec((None, KT, 2 * SP), lambda s: (s, 0, 0)),
                   pl.BlockSpec((None, 2 * SP, KT), lambda s: (s, 0, 0)),
                   pl.BlockSpec((None, 1, 2 * SP), lambda s: (s, 0, 0))],
        out_shape=[jax.ShapeDtypeStruct((N_SLAB, KT, 2 * LANES), BF16),
                   jax.ShapeDtypeStruct((N_SLAB, KT, 2 * SP), BF16),
                   jax.ShapeDtypeStruct((N_SLAB, 2 * SP, KT), BF16),
                   jax.ShapeDtypeStruct((N_SLAB, 1, 2 * SP), F32)],
        compiler_params=_cparams(("arbitrary",), 40 << 20),
        name="s5_prep_bwd" if reverse else "s5_prep_fwd",
    )(rows(lam_re), rows(lam_im), ls, b_blockdiag(b_re), b_blockdiag(b_im),
      c_blockdiag(c_re), c_blockdiag(c_im))


def _s5_scan_kernel(*refs, reverse, second):
    it = iter(refs)
    ul_ref = next(it)
    uc_ref = next(it)
    y1_ref = next(it) if second else None
    tall_ref, w_ref, v_ref, at_ref = next(it), next(it), next(it), next(it)
    dsk_ref = None if second else next(it)
    o_ref = next(it)
    x2_ref, xc2_ref, xst_ref, sprev_ref, state_ref = it

    rows = JB * BATCH
    crow = J_CTX * BATCH
    a_re = at_ref[:, 0:SP]
    a_im = at_ref[:, SP:2 * SP]

    def token(ref, t, n):
        return ref[pl.ds(t, n, stride=T_CHUNK), :]

    def scan(n_chunks, keep):
        s_re = state_ref[:, 0:SP]
        s_im = state_ref[:, SP:2 * SP]
        order = range(n_chunks - 1, -1, -1) if reverse else range(n_chunks)
        for jj in order:
            r = slice(jj * BATCH, (jj + 1) * BATCH)
            if keep:
                sprev_ref[r, 0:SP] = s_re
                sprev_ref[r, SP:2 * SP] = s_im
            x_re = xst_ref[r, 0:SP]
            x_im = xst_ref[r, SP:2 * SP]
            s_re, s_im = (a_re * s_re - a_im * s_im + x_re,
                          a_re * s_im + a_im * s_re + x_im)
        state_ref[:, 0:SP] = s_re
        state_ref[:, SP:2 * SP] = s_im

    @pl.when(pl.program_id(1) == 0)
    def _context():
        state_ref[...] = jnp.zeros_like(state_ref)
        for t in range(T_CHUNK):
            xc2_ref[:, t * LANES:(t + 1) * LANES] = token(uc_ref, t, crow).astype(BF16)
        xst_ref[0:crow, :] = jnp.dot(xc2_ref[...], w_ref[...], preferred_element_type=F32)
        scan(J_CTX, keep=False)

    for t in range(T_CHUNK):
        x2_ref[:, t * LANES:(t + 1) * LANES] = token(ul_ref, t, rows).astype(BF16)
    xst_ref[...] = jnp.dot(x2_ref[...], w_ref[...], preferred_element_type=F32)
    scan(JB, keep=True)
    sp = sprev_ref[...].astype(BF16)

    n_tiles = KT // (2 * LANES)
    for nt in range(n_tiles):
        cols = slice(nt * 2 * LANES, (nt + 1) * 2 * LANES)
        if reverse:
            xa = x2_ref[:, nt * 2 * LANES:]
            ta = tall_ref[0:(n_tiles - nt) * 2 * LANES, :]
        else:
            xa = x2_ref[:, 0:(nt + 1) * 2 * LANES]
            ta = tall_ref[(n_tiles - 1 - nt) * 2 * LANES:, :]
        y = (jnp.dot(xa, ta, preferred_element_type=F32)
             + jnp.dot(sp, v_ref[:, cols], preferred_element_type=F32))
        for c in range(2):
            t = 2 * nt + c
            yt = y[:, c * LANES:(c + 1) * LANES]
            if second:
                val = jax.nn.gelu(token(y1_ref, t, rows) + yt)
            else:
                val = dsk_ref[...] * token(ul_ref, t, rows) + yt
            o_ref[pl.ds(t, rows, stride=T_CHUNK), :] = val


def _s5_scan(u_lat, u_ctx, ops, *, reverse, y_prev=None, d_skip=None):
    tall, w, v, at = ops
    second = y_prev is not None
    blk_rows = JB * BATCH * T_CHUNK
    ctx_rows = J_CTX * BATCH * T_CHUNK

    def jblk(q):
        return (N_JB - 1 - q) if reverse else q

    lat_spec = pl.BlockSpec((blk_rows, LANES), lambda s, q: (jblk(q), s))
    in_specs = [lat_spec, pl.BlockSpec((ctx_rows, LANES), lambda s, q: (0, s))]
    args = [u_lat.reshape(J_LAT * BATCH * T_CHUNK, D_MODEL), u_ctx.reshape(ctx_rows, D_MODEL)]
    if second:
        in_specs.append(lat_spec)
        args.append(y_prev.reshape(J_LAT * BATCH * T_CHUNK, D_MODEL))
    in_specs += [pl.BlockSpec((None, KT, 2 * LANES), lambda s, q: (s, 0, 0)),
                 pl.BlockSpec((None, KT, 2 * SP), lambda s, q: (s, 0, 0)),
                 pl.BlockSpec((None, 2 * SP, KT), lambda s, q: (s, 0, 0)),
                 pl.BlockSpec((None, 1, 2 * SP), lambda s, q: (s, 0, 0))]
    args += [tall, w, v, at]
    if not second:
        in_specs.append(pl.BlockSpec((None, None, 1, LANES), lambda s, q: (0, s, 0, 0)))
        args.append(d_skip.reshape(1, N_SLAB, 1, LANES))
    rows = JB * BATCH
    out = pl.pallas_call(
        functools.partial(_s5_scan_kernel, reverse=reverse, second=second),
        grid=(N_SLAB, N_JB),
        in_specs=in_specs,
        out_specs=lat_spec,
        out_shape=jax.ShapeDtypeStruct((J_LAT * BATCH * T_CHUNK, D_MODEL), F32),
        scratch_shapes=[pltpu.VMEM((rows, KT), BF16),
                        pltpu.VMEM((J_CTX * BATCH, KT), BF16),
                        pltpu.VMEM((rows, 2 * SP), F32),
                        pltpu.VMEM((rows, 2 * SP), F32),
                        pltpu.VMEM((BATCH, 2 * SP), F32)],
        compiler_params=_cparams(("arbitrary", "arbitrary"), VMEM_CEILING_BYTES),
        name="s5_scan_bwd" if reverse else "s5_scan_fwd",
    )(*args)
    return out.reshape(J_LAT, BATCH, T_CHUNK, D_MODEL)


def kernel(x, c, ctx, c_ctx, w_mod, b_mod, ln_g, ln_b, w_in_a, lam_re, lam_im, log_step,
           b_re, b_im, c_re, c_im, d_skip, w_glu_a, w_out_a, w_in_b, conv_w, w_out_b, w1, w3, w2):
    d = D_MODEL
    d_ff = w1.shape[-1]
    m = BATCH * SEQ

    c_rows = jnp.concatenate([c, c_ctx[None, :], jnp.zeros((MOD_ROWS - BATCH - 1, d), F32)], axis=0)
    mods = [_mod_rows(c_rows, w_mod, b_mod, i) for i in range(DEPTH)]

    def ffn(y, layer):
        h, mu, rs = _norm(y, ln_g, ln_b, layer, 0, mods[layer], 3, 4)
        g = _matmul(h, [(w1, layer, 0), (w3, layer, 0)], d_ff, tm=2048, tn=256, epilogue="swiglu",
                    out_dtype=BF16, single_buffer_lhs=True, name="mm_swiglu")
        return _matmul_resid(g, w2, layer, (y, mu, rs), mods[layer], 5, tm=1024, tn=256, prev_ln=(layer, 0),
                             ln_g=ln_g, ln_b=ln_b, name="mm_w2")

    h_l = _modulate(x, mods[0], 0, 1)
    h_c = _modulate(ctx, mods[0], 0, 1, fixed_row=BATCH)
    u_l = _matmul(h_l, [(w_in_a, 0, 0)], d, tm=1024, tn=512, epilogue="plain", out_dtype=F32,
                  chunk_tokens=SEQ, name="mm_in_a")
    u_c = _matmul(h_c, [(w_in_a, 0, 0)], d, tm=BATCH * CTX_LEN, tn=512, epilogue="plain", out_dtype=F32,
                  chunk_tokens=CTX_LEN, single_buffer_lhs=True, name="mm_in_a_ctx")
    ops_f = _s5_prep(lam_re[0, 0], lam_im[0, 0], log_step[0, 0], b_re[0, 0], b_im[0, 0],
                     c_re[0, 0], c_im[0, 0], reverse=False)
    ops_b = _s5_prep(lam_re[0, 1], lam_im[0, 1], log_step[0, 1], b_re[0, 1], b_im[0, 1],
                     c_re[0, 1], c_im[0, 1], reverse=True)
    y_f = _s5_scan(u_l, u_c, ops_f, reverse=False, d_skip=d_skip)
    z = _s5_scan(u_l, u_c, ops_b, reverse=True, y_prev=y_f)
    gz = _matmul(z, [(w_glu_a, 0, 0)], d, tm=1024, tn=512, epilogue="glu", out_dtype=BF16,
                 chunked_lhs=True, single_buffer_lhs=True, name="mm_glu")
    y = _matmul_resid(gz, w_out_a, 0, x.reshape(m, d), mods[0], 2, tm=2048, tn=256, name="mm_out_a")
    y = ffn(y, 0)

    h, mu, rs = _norm(y, ln_g, ln_b, 0, 1, mods[1], 0, 1)
    cw_spec = pl.BlockSpec((None, 3, 256), lambda i, j: (0, 0, j))
    gz = _matmul(h, [(w_in_b, 0, 0), (w_in_b, 0, d), (w_in_b, 0, 2 * d)], d, tm=1024, tn=256, epilogue="conv",
                 out_dtype=BF16, extra=(conv_w,), extra_specs=(cw_spec,), single_buffer_lhs=True, name="mm_conv")
    y = _matmul_resid(gz, w_out_b, 0, (y, mu, rs), mods[1], 2, tm=2048, tn=256, prev_ln=(0, 1),
                      ln_g=ln_g, ln_b=ln_b, name="mm_out_b")
    y = ffn(y, 1)
    return _norm(y, ln_g, ln_b, 1, 1).reshape(BATCH, SEQ, d)
```

```python
import functools

import jax
import jax.numpy as jnp
from jax import lax
from jax.experimental import pallas as pl
from jax.experimental.pallas import tpu as pltpu

F32 = jnp.float32
BF16 = jnp.bfloat16

D_MODEL = 4096
BATCH = 8
SEQ = 2048
CTX_LEN = 256
GROUP = 16
STATE = 64
GRID_W = 64
DEPTH = 2
LN_EPS = 1e-5
ALPHA = (2.0 * DEPTH) ** 0.25
MOD_ROWS = 16
N_MOD = 6

LANES = 128
SUBLANES = 8
VMEM_CEILING_BYTES = 60 * 1024 * 1024

T_CHUNK = 16
N_SLAB = D_MODEL // LANES
G_SLAB = LANES // GROUP
SP = G_SLAB * STATE
J_LAT = SEQ // T_CHUNK
J_CTX = CTX_LEN // T_CHUNK
JB = 64
N_JB = J_LAT // JB
KT = T_CHUNK * LANES
POW_ROWS = 32
NORM_ROWS = 16
NORM_UNROLL = 8


def _cparams(sem, nbytes):
    limit = int(min(VMEM_CEILING_BYTES, max(32 * 1024 * 1024, nbytes)))
    return pltpu.CompilerParams(dimension_semantics=sem, vmem_limit_bytes=limit)


def _mod_spec(k, grid_rank):
    if grid_rank == 1:
        return pl.BlockSpec((None, MOD_ROWS, D_MODEL), lambda i: (k, 0, 0))
    return pl.BlockSpec((None, MOD_ROWS, D_MODEL), lambda i, j: (k, 0, 0))


def _mod_kernel(c_ref, w_ref, b_ref, o_ref):
    c = c_ref[...]
    s = (c * jax.nn.sigmoid(c)).astype(BF16)
    o_ref[...] = jnp.dot(s, w_ref[...].astype(BF16), preferred_element_type=F32) + b_ref[...]


def _mod_rows(c_rows, w_mod, b_mod, layer):
    m, k = c_rows.shape
    n = w_mod.shape[2]
    tn = 512
    per = D_MODEL // tn
    return pl.pallas_call(
        _mod_kernel,
        grid=(n // tn,),
        in_specs=[pl.BlockSpec((m, k), lambda j: (0, 0)),
                  pl.BlockSpec((None, k, tn), lambda j: (layer, 0, j)),
                  pl.BlockSpec((None, 1, tn), lambda j: (layer, 0, j))],
        out_specs=pl.BlockSpec((None, m, tn), lambda j: (j // per, 0, j % per)),
        out_shape=jax.ShapeDtypeStruct((N_MOD, m, D_MODEL), F32),
        compiler_params=_cparams(("arbitrary",), 3 * k * tn * 4),
        name="mod_rows",
    )(c_rows, w_mod, b_mod.reshape(DEPTH, 1, n))


def _modulate_kernel(x_ref, sh_ref, sc_ref, o_ref, *, per, fixed_row):
    row = fixed_row if fixed_row is not None else pl.program_id(0) // per
    sh = sh_ref[pl.ds(row, 1), :]
    sc = sc_ref[pl.ds(row, 1), :]
    o_ref[...] = (x_ref[...] * (1.0 + sc) + sh).astype(o_ref.dtype)


def _modulate(x, mod, k_shift, k_scale, fixed_row=None, tr=512):
    b, n, d = x.shape
    tr = min(tr, n)
    per = n // tr
    return pl.pallas_call(
        functools.partial(_modulate_kernel, per=per, fixed_row=fixed_row),
        grid=(b * per,),
        in_specs=[pl.BlockSpec((None, tr, d), lambda i: (i // per, i % per, 0)),
                  _mod_spec(k_shift, 1), _mod_spec(k_scale, 1)],
        out_specs=pl.BlockSpec((tr, d), lambda i: (i, 0)),
        out_shape=jax.ShapeDtypeStruct((b * n, d), BF16),
        compiler_params=_cparams(("arbitrary",), 2 * tr * d * 6 + (8 << 20)),
        name="modulate",
    )(x, mod, mod)


def _norm_kernel(*refs, with_mod, per):
    if with_mod:
        y_ref, lg_ref, lb_ref, sh_ref, sc_ref, o_ref, mu_ref, rs_ref = refs
    else:
        y_ref, lg_ref, lb_ref, o_ref = refs
    lg = lg_ref[...]
    lb = lb_ref[...]
    if with_mod:
        b = pl.program_id(0) // per
        sc1 = 1.0 + sc_ref[pl.ds(b, 1), :]
        sh = sh_ref[pl.ds(b, 1), :]

    def rows(c, carry):
        r = pl.ds(pl.multiple_of(c * NORM_ROWS, NORM_ROWS), NORM_ROWS)
        y = y_ref[r, :]
        mu = jnp.mean(y, axis=-1, keepdims=True)
        yc = y - mu
        rs = lax.rsqrt(jnp.mean(yc * yc, axis=-1, keepdims=True) + LN_EPS)
        xn = yc * rs * lg + lb
        if with_mod:
            xn = xn * sc1 + sh
            mu_ref[r, :] = jnp.broadcast_to(mu, (NORM_ROWS, LANES))
            rs_ref[r, :] = jnp.broadcast_to(rs, (NORM_ROWS, LANES))
        o_ref[r, :] = xn.astype(o_ref.dtype)
        return carry

    lax.fori_loop(0, y_ref.shape[0] // NORM_ROWS, rows, 0, unroll=NORM_UNROLL)


def _ln_vec_spec(layer, sub, width, grid_rank):
    if grid_rank == 1:
        return pl.BlockSpec((None, None, 1, width), lambda i: (layer, sub, 0, 0))
    return pl.BlockSpec((None, None, 1, width), lambda i, j: (layer, sub, 0, j))


def _norm(y, ln_g, ln_b, layer, sub, mod=None, k_shift=None, k_scale=None, tr=512):
    m, d = y.shape
    per = SEQ // tr
    with_mod = mod is not None
    row = pl.BlockSpec((tr, d), lambda i: (i, 0))
    stat = pl.BlockSpec((tr, LANES), lambda i: (i, 0))
    vec = _ln_vec_spec(layer, sub, d, 1)
    in_specs = [row, vec, vec]
    args = [y, ln_g.reshape(DEPTH, 2, 1, d), ln_b.reshape(DEPTH, 2, 1, d)]
    if with_mod:
        in_specs += [_mod_spec(k_shift, 1), _mod_spec(k_scale, 1)]
        args += [mod, mod]
        out_specs = [row, stat, stat]
        out_shape = [jax.ShapeDtypeStruct((m, d), BF16), jax.ShapeDtypeStruct((m, LANES), F32),
                     jax.ShapeDtypeStruct((m, LANES), F32)]
    else:
        out_specs = row
        out_shape = jax.ShapeDtypeStruct((m, d), F32)
    return pl.pallas_call(
        functools.partial(_norm_kernel, with_mod=with_mod, per=per),
        grid=(m // tr,),
        in_specs=in_specs, out_specs=out_specs, out_shape=out_shape,
        compiler_params=_cparams(("arbitrary",), 2 * tr * d * 12 + (8 << 20)),
        name="norm_mod" if with_mod else "norm_out",
    )(*args)


def _mm_resid_kernel(*refs, has_stats, per):
    if has_stats:
        lhs_ref, w_ref, prev_ref, mu_ref, rs_ref, lg_ref, lb_ref, g_ref, y_ref = refs
    else:
        lhs_ref, w_ref, prev_ref, g_ref, y_ref = refs
    acc = jnp.dot(lhs_ref[...], w_ref[...].astype(BF16), preferred_element_type=F32)
    prev = prev_ref[...]
    if has_stats:
        prev = (prev - mu_ref[:, 0:1]) * rs_ref[:, 0:1] * lg_ref[...] + lb_ref[...]
    y_ref[...] = ALPHA * prev + g_ref[pl.ds(pl.program_id(0) // per, 1), :] * acc


def _matmul_resid(lhs, w, layer, prev, gate_mod, k_gate, *, tm, tn, prev_ln=None, ln_g=None, ln_b=None,
                  name="mm_resid"):
    m, k = lhs.shape
    d = D_MODEL
    per = SEQ // tm
    has_stats = prev_ln is not None
    tile = pl.BlockSpec((tm, tn), lambda i, j: (i, j))
    in_specs = [pl.BlockSpec((tm, k), lambda i, j: (i, 0), pipeline_mode=pl.Buffered(1)),
                pl.BlockSpec((None, k, tn), lambda i, j: (layer, 0, j)), tile]
    if has_stats:
        y_prev, mu_prev, rs_prev = prev
        vec = _ln_vec_spec(prev_ln[0], prev_ln[1], tn, 2)
        stat_in = pl.BlockSpec((tm, LANES), lambda i, j: (i, 0), pipeline_mode=pl.Buffered(1))
        in_specs += [stat_in, stat_in, vec, vec]
        args = [lhs, w, y_prev, mu_prev, rs_prev, ln_g.reshape(DEPTH, 2, 1, d), ln_b.reshape(DEPTH, 2, 1, d)]
    else:
        args = [lhs, w, prev]
    in_specs.append(pl.BlockSpec((None, MOD_ROWS, tn), lambda i, j: (k_gate, 0, j)))
    args.append(gate_mod)
    nbytes = (tm * k * lhs.dtype.itemsize + k * tn * (2 * w.dtype.itemsize + 2) + tm * tn * 4 * 8
              + tm * LANES * 4 * 2 + (4 << 20))
    return pl.pallas_call(
        functools.partial(_mm_resid_kernel, has_stats=has_stats, per=per),
        grid=(m // tm, d // tn),
        in_specs=in_specs,
        out_specs=tile,
        out_shape=jax.ShapeDtypeStruct((m, d), F32),
        compiler_params=_cparams(("arbitrary", "arbitrary"), nbytes),
        name=name,
    )(*args)


def _mm_kernel(*refs, n_w, epilogue, chunked_lhs, out_batches, tn, n_side):
    lhs_ref = refs[0]
    w_refs = refs[1:1 + n_w]
    n_extra = 1 if epilogue == "conv" else 0
    extra = refs[1 + n_w:1 + n_w + n_extra]
    n_in = 1 + n_w + n_extra + n_side
    side_src = refs[n_in - n_side:n_in]
    o_ref = refs[n_in]
    side_dst = refs[n_in + 1:n_in + 1 + n_side]
    scratch = refs[n_in + 1 + n_side:]
    j = pl.program_id(1)
    for src, dst in zip(side_src, side_dst):
        dst[...] = src[...].astype(BF16)
    if chunked_lhs:
        lhs_b_ref, = scratch
        tm, k = lhs_b_ref.shape

        @pl.when(j == 0)
        def _cast():
            lhs_b_ref[...] = lhs_ref[...].reshape(tm, k).astype(BF16)

        lhs_b = lhs_b_ref[...]
    else:
        lhs_b = lhs_ref[...]
    accs = [jnp.dot(lhs_b, w[...].astype(BF16), preferred_element_type=F32) for w in w_refs]
    if epilogue == "plain":
        out = accs[0]
    elif epilogue == "glu":
        zt = lhs_ref[:, :, pl.ds(pl.multiple_of(j * tn, tn), tn)]
        out = zt.reshape(accs[0].shape) * jax.nn.sigmoid(accs[0])
    elif epilogue == "swiglu":
        a1, a3 = accs
        out = a1 * jax.nn.sigmoid(a1) * a3
    elif epilogue == "conv":
        gb, gc, v = accs
        cw_ref, = extra
        t = gc * v
        tm = t.shape[0]
        pos = lax.broadcasted_iota(jnp.int32, t.shape, 0) % GRID_W
        prev = jnp.where(pos == 0, 0.0, pltpu.roll(t, 1, 0))
        nxt = jnp.where(pos == GRID_W - 1, 0.0, pltpu.roll(t, tm - 1, 0))
        out = gb * (cw_ref[0:1, :] * prev + cw_ref[1:2, :] * t + cw_ref[2:3, :] * nxt)
    else:
        raise ValueError(epilogue)
    out = out.astype(o_ref.dtype)
    if out_batches is None:
        o_ref[...] = out
    elif out_batches == 1:
        o_ref[...] = out.reshape(o_ref.shape)
    else:
        n_tok = out.shape[0] // out_batches
        for bb in range(out_batches):
            o_ref[:, bb, :, :] = out[bb * n_tok:(bb + 1) * n_tok].reshape(n_tok // T_CHUNK, T_CHUNK, tn)


def _matmul(lhs, weights, n_out, *, tm, tn, epilogue, out_dtype, chunked_lhs=False, chunk_tokens=None,
            extra=(), extra_specs=(), single_buffer_lhs=False, side_casts=(), name="matmul"):
    k = weights[0][0].shape[1]
    buffered = dict(pipeline_mode=pl.Buffered(1)) if single_buffer_lhs else {}
    scratch_shapes = []
    if chunked_lhs:
        jn, bn, tt, _ = lhs.shape
        m = jn * bn * tt
        per = (jn * tt) // tm
        lhs_spec = pl.BlockSpec((tm // tt, None, tt, k), lambda i, j: (i % per, i // per, 0, 0), **buffered)
        scratch_shapes.append(pltpu.VMEM((tm, k), BF16))
    else:
        m = lhs.shape[0]
        lhs_spec = pl.BlockSpec((tm, k), lambda i, j: (i, 0), **buffered)
    w_specs = [pl.BlockSpec((None, k, tn), functools.partial(lambda i, j, layer, blk: (layer, 0, j + blk),
                                                             layer=layer, blk=off // tn))
               for (_, layer, off) in weights]
    out_batches = None
    if chunk_tokens is not None:
        jn = chunk_tokens // T_CHUNK
        if tm <= chunk_tokens:
            out_batches = 1
            per_o = chunk_tokens // tm
            out_spec = pl.BlockSpec((tm // T_CHUNK, None, T_CHUNK, tn), lambda i, j: (i % per_o, i // per_o, 0, j))
        else:
            out_batches = tm // chunk_tokens
            out_spec = pl.BlockSpec((jn, out_batches, T_CHUNK, tn), lambda i, j: (0, i, 0, j))
        out_shape = jax.ShapeDtypeStruct((jn, BATCH, T_CHUNK, n_out), out_dtype)
    else:
        out_spec = pl.BlockSpec((tm, tn), lambda i, j: (i, j))
        out_shape = jax.ShapeDtypeStruct((m, n_out), out_dtype)
    w_arrays = [w for (w, _, _) in weights]
    lhs_bytes = tm * k * lhs.dtype.itemsize * (1 if single_buffer_lhs else 2) + (tm * k * 2 if chunked_lhs else 0)
    w_bytes = sum(2 * k * tn * w.dtype.itemsize + (k * tn * 2 if w.dtype != BF16 else 0) for w in w_arrays)
    out_bytes = tm * tn * (2 * jnp.dtype(out_dtype).itemsize + 4 * (len(weights) + 3))
    n_j = n_out // tn
    side_in_specs, side_out_specs, side_out_shapes, side_srcs, side_bytes = [], [], [], [], 0
    for src, rows, first, n_blocks in side_casts:
        cols = src.shape[1]

        def blk(i, j, n_blocks=n_blocks):
            return jnp.minimum(i * n_j + j, n_blocks - 1)

        side_in_specs.append(pl.BlockSpec((rows, cols), lambda i, j, blk=blk, first=first: (blk(i, j) + first, 0)))
        side_out_specs.append(pl.BlockSpec((rows, cols), lambda i, j, blk=blk: (blk(i, j), 0)))
        side_out_shapes.append(jax.ShapeDtypeStruct((rows * n_blocks, cols), BF16))
        side_srcs.append(src)
        side_bytes += 2 * rows * cols * 6
    res = pl.pallas_call(
        functools.partial(_mm_kernel, n_w=len(weights), epilogue=epilogue, chunked_lhs=chunked_lhs,
                          out_batches=out_batches, tn=tn, n_side=len(side_srcs)),
        grid=(m // tm, n_j),
        in_specs=[lhs_spec] + w_specs + list(extra_specs) + side_in_specs,
        out_specs=[out_spec] + side_out_specs, out_shape=[out_shape] + side_out_shapes,
        scratch_shapes=scratch_shapes,
        compiler_params=_cparams(("arbitrary", "arbitrary"),
                                 lhs_bytes + w_bytes + out_bytes + side_bytes + (4 << 20)),
        name=name,
    )(lhs, *w_arrays, *extra, *side_srcs)
    return res if side_srcs else res[0]


def _s5_prep_kernel(lr_ref, li_ref, ls_ref, bre_ref, bim_ref, cre_ref, cim_ref,
                    tall_ref, w_ref, v_ref, at_ref, *, reverse):
    lr = lr_ref[...]
    li = li_ref[...]
    dt = jnp.exp(ls_ref[...])
    kk = jnp.minimum(lax.broadcasted_iota(jnp.int32, (POW_ROWS, SP), 0), T_CHUNK).astype(F32)
    mag = jnp.exp(kk * (lr * dt))
    ang = kk * (li * dt)
    p_re = mag * jnp.cos(ang)
    p_im = mag * jnp.sin(ang)
    a_re = p_re[1:2, :]
    a_im = p_im[1:2, :]
    n_re, n_im = a_re - 1.0, a_im
    den = lr * lr + li * li
    f_re = (n_re * lr + n_im * li) / den
    f_im = (n_im * lr - n_re * li) / den
    b_re = bre_ref[...]
    b_im = bim_ref[...]
    bb_re = f_re * b_re - f_im * b_im
    bb_im = f_re * b_im + f_im * b_re
    c_re = cre_ref[...]
    c_im = cim_ref[...]
    c_re_b = c_re.astype(BF16)
    c_im_b = c_im.astype(BF16)

    dks = []
    for k in range(T_CHUNK):
        pr, pi = p_re[k:k + 1, :], p_im[k:k + 1, :]
        bk_re = (bb_re * pr - bb_im * pi).astype(BF16)
        bk_im = (bb_re * pi + bb_im * pr).astype(BF16)
        dk = (jnp.dot(bk_re, c_re_b, preferred_element_type=F32)
              - jnp.dot(bk_im, c_im_b, preferred_element_type=F32))
        dks.append(dk.astype(BF16))
        m = k if reverse else T_CHUNK - 1 - k
        w_ref[m * LANES:(m + 1) * LANES, 0:SP] = bk_re
        w_ref[m * LANES:(m + 1) * LANES, SP:2 * SP] = bk_im

    zero = jnp.zeros((LANES, LANES), BF16)
    for r in range(T_CHUNK):
        for c in range(2):
            idx = (r - c) if reverse else (T_CHUNK - 2 - r + c)
            blk = dks[idx] if 0 <= idx < T_CHUNK else zero
            tall_ref[r * LANES:(r + 1) * LANES, c * LANES:(c + 1) * LANES] = blk

    pad = jnp.zeros((LANES - POW_ROWS, SP), F32)
    pt_re = jnp.concatenate([p_re, pad], axis=0).T
    pt_im = jnp.concatenate([p_im, pad], axis=0).T
    for t in range(T_CHUNK):
        k = (T_CHUNK - t) if reverse else (t + 1)
        pr, pi = pt_re[:, k:k + 1], pt_im[:, k:k + 1]
        v_ref[0:SP, t * LANES:(t + 1) * LANES] = (pr * c_re - pi * c_im).astype(BF16)
        v_ref[SP:2 * SP, t * LANES:(t + 1) * LANES] = (-(pr * c_im + pi * c_re)).astype(BF16)

    at_ref[:, 0:SP] = p_re[T_CHUNK:T_CHUNK + 1, :]
    at_ref[:, SP:2 * SP] = p_im[T_CHUNK:T_CHUNK + 1, :]


def _s5_prep(lam_re, lam_im, log_step, b_re, b_im, c_re, c_im, *, reverse):
    def rows(v):
        return v.reshape(N_SLAB, 1, SP)

    eye = jnp.eye(G_SLAB, dtype=bool)

    def b_blockdiag(b):
        bt = b.reshape(N_SLAB, G_SLAB, STATE, GROUP).transpose(0, 1, 3, 2)
        bd = jnp.where(eye[None, :, None, :, None], bt[:, :, :, None, :], 0.0)
        return bd.reshape(N_SLAB, LANES, SP)

    def c_blockdiag(c):
        ct = c.reshape(N_SLAB, G_SLAB, GROUP, STATE).transpose(0, 1, 3, 2)
        cd = jnp.where(eye[None, :, None, :, None], ct[:, :, :, None, :], 0.0)
        return cd.reshape(N_SLAB, SP, LANES)

    ls = jnp.repeat(log_step, STATE).reshape(N_SLAB, 1, SP)
    vec = pl.BlockSpec((None, 1, SP), lambda s: (s, 0, 0))
    bsp = pl.BlockSpec((None, LANES, SP), lambda s: (s, 0, 0))
    csp = pl.BlockSpec((None, SP, LANES), lambda s: (s, 0, 0))
    return pl.pallas_call(
        functools.partial(_s5_prep_kernel, reverse=reverse),
        grid=(N_SLAB,),
        in_specs=[vec, vec, vec, bsp, bsp, csp, csp],
        out_specs=[pl.BlockSpec((None, KT, 2 * LANES), lambda s: (s, 0, 0)),
                   pl.BlockSpec((None, KT, 2 * SP), lambda s: (s, 0, 0)),
                   pl.BlockSpec((None, 2 * SP, KT), lambda s: (s, 0, 0)),
                   pl.BlockSpec((None, 1, 2 * SP), lambda s: (s, 0, 0))],
        out_shape=[jax.ShapeDtypeStruct((N_SLAB, KT, 2 * LANES), BF16),
                   jax.ShapeDtypeStruct((N_SLAB, KT, 2 * SP), BF16),
                   jax.ShapeDtypeStruct((N_SLAB, 2 * SP, KT), BF16),
                   jax.ShapeDtypeStruct((N_SLAB, 1, 2 * SP), F32)],
        compiler_params=_cparams(("arbitrary",), 40 << 20),
        name="s5_prep_bwd" if reverse else "s5_prep_fwd",
    )(rows(lam_re), rows(lam_im), ls, b_blockdiag(b_re), b_blockdiag(b_im),
      c_blockdiag(c_re), c_blockdiag(c_im))


def _s5_scan_kernel(*refs, reverse, second):
    it = iter(refs)
    ul_ref = next(it)
    uc_ref = next(it)
    y1_ref = next(it) if second else None
    tall_ref, w_ref, v_ref, at_ref = next(it), next(it), next(it), next(it)
    dsk_ref = None if second else next(it)
    o_ref = next(it)
    x2_ref, xc2_ref, xst_ref, sprev_ref, state_ref = it

    rows = JB * BATCH
    crow = J_CTX * BATCH
    a_re = at_ref[:, 0:SP]
    a_im = at_ref[:, SP:2 * SP]

    def token(ref, t, n):
        return ref[pl.ds(t, n, stride=T_CHUNK), :]

    def scan(n_chunks, keep):
        s_re = state_ref[:, 0:SP]
        s_im = state_ref[:, SP:2 * SP]
        order = range(n_chunks - 1, -1, -1) if reverse else range(n_chunks)
        for jj in order:
            r = slice(jj * BATCH, (jj + 1) * BATCH)
            if keep:
                sprev_ref[r, 0:SP] = s_re
                sprev_ref[r, SP:2 * SP] = s_im
            x_re = xst_ref[r, 0:SP]
            x_im = xst_ref[r, SP:2 * SP]
            s_re, s_im = (a_re * s_re - a_im * s_im + x_re,
                          a_re * s_im + a_im * s_re + x_im)
        state_ref[:, 0:SP] = s_re
        state_ref[:, SP:2 * SP] = s_im

    @pl.when(pl.program_id(1) == 0)
    def _context():
        state_ref[...] = jnp.zeros_like(state_ref)
        for t in range(T_CHUNK):
            xc2_ref[:, t * LANES:(t + 1) * LANES] = token(uc_ref, t, crow).astype(BF16)
        xst_ref[0:crow, :] = jnp.dot(xc2_ref[...], w_ref[...], preferred_element_type=F32)
        scan(J_CTX, keep=False)

    for t in range(T_CHUNK):
        x2_ref[:, t * LANES:(t + 1) * LANES] = token(ul_ref, t, rows).astype(BF16)
    xst_ref[...] = jnp.dot(x2_ref[...], w_ref[...], preferred_element_type=F32)
    scan(JB, keep=True)
    sp = sprev_ref[...].astype(BF16)

    n_tiles = KT // (2 * LANES)
    for nt in range(n_tiles):
        cols = slice(nt * 2 * LANES, (nt + 1) * 2 * LANES)
        if reverse:
            xa = x2_ref[:, nt * 2 * LANES:]
            ta = tall_ref[0:(n_tiles - nt) * 2 * LANES, :]
        else:
            xa = x2_ref[:, 0:(nt + 1) * 2 * LANES]
            ta = tall_ref[(n_tiles - 1 - nt) * 2 * LANES:, :]
        y = (jnp.dot(xa, ta, preferred_element_type=F32)
             + jnp.dot(sp, v_ref[:, cols], preferred_element_type=F32))
        for c in range(2):
            t = 2 * nt + c
            yt = y[:, c * LANES:(c + 1) * LANES]
            if second:
                val = jax.nn.gelu(token(y1_ref, t, rows) + yt)
            else:
                val = dsk_ref[...] * token(ul_ref, t, rows) + yt
            o_ref[pl.ds(t, rows, stride=T_CHUNK), :] = val


def _s5_scan(u_lat, u_ctx, ops, *, reverse, y_prev=None, d_skip=None):
    tall, w, v, at = ops
    second = y_prev is not None
    blk_rows = JB * BATCH * T_CHUNK
    ctx_rows = J_CTX * BATCH * T_CHUNK

    def jblk(q):
        return (N_JB - 1 - q) if reverse else q

    lat_spec = pl.BlockSpec((blk_rows, LANES), lambda s, q: (jblk(q), s))
    in_specs = [lat_spec, pl.BlockSpec((ctx_rows, LANES), lambda s, q: (0, s))]
    args = [u_lat.reshape(J_LAT * BATCH * T_CHUNK, D_MODEL), u_ctx.reshape(ctx_rows, D_MODEL)]
    if second:
        in_specs.append(lat_spec)
        args.append(y_prev.reshape(J_LAT * BATCH * T_CHUNK, D_MODEL))
    in_specs += [pl.BlockSpec((None, KT, 2 * LANES), lambda s, q: (s, 0, 0)),
                 pl.BlockSpec((None, KT, 2 * SP), lambda s, q: (s, 0, 0)),
                 pl.BlockSpec((None, 2 * SP, KT), lambda s, q: (s, 0, 0)),
                 pl.BlockSpec((None, 1, 2 * SP), lambda s, q: (s, 0, 0))]
    args += [tall, w, v, at]
    if not second:
        in_specs.append(pl.BlockSpec((None, None, 1, LANES), lambda s, q: (0, s, 0, 0)))
        args.append(d_skip.reshape(1, N_SLAB, 1, LANES))
    rows = JB * BATCH
    out = pl.pallas_call(
        functools.partial(_s5_scan_kernel, reverse=reverse, second=second),
        grid=(N_SLAB, N_JB),
        in_specs=in_specs,
        out_specs=lat_spec,
        out_shape=jax.ShapeDtypeStruct((J_LAT * BATCH * T_CHUNK, D_MODEL), F32),
        scratch_shapes=[pltpu.VMEM((rows, KT), BF16),
                        pltpu.VMEM((J_CTX * BATCH, KT), BF16),
                        pltpu.VMEM((rows, 2 * SP), F32),
                        pltpu.VMEM((rows, 2 * SP), F32),
                        pltpu.VMEM((BATCH, 2 * SP), F32)],
        compiler_params=_cparams(("arbitrary", "arbitrary"), VMEM_CEILING_BYTES),
        name="s5_scan_bwd" if reverse else "s5_scan_fwd",
    )(*args)
    return out.reshape(J_LAT, BATCH, T_CHUNK, D_MODEL)


def kernel(x, c, ctx, c_ctx, w_mod, b_mod, ln_g, ln_b, w_in_a, lam_re, lam_im, log_step,
           b_re, b_im, c_re, c_im, d_skip, w_glu_a, w_out_a, w_in_b, conv_w, w_out_b, w1, w3, w2):
    d = D_MODEL
    d_ff = w1.shape[-1]
    m = BATCH * SEQ

    c_rows = jnp.concatenate([c, c_ctx[None, :], jnp.zeros((MOD_ROWS - BATCH - 1, d), F32)], axis=0)
    mods = [_mod_rows(c_rows, w_mod, b_mod, i) for i in range(DEPTH)]

    def w2_resid(g, w2_any, layer, stream):
        return _matmul_resid(g, w2_any, layer, stream, mods[layer], 5, tm=1024, tn=512, prev_ln=(layer, 0),
                             ln_g=ln_g, ln_b=ln_b, name="mm_w2")

    h_l = _modulate(x, mods[0], 0, 1)
    h_c = _modulate(ctx, mods[0], 0, 1, fixed_row=BATCH)
    u_l = _matmul(h_l, [(w_in_a, 0, 0)], d, tm=1024, tn=512, epilogue="plain", out_dtype=F32,
                  chunk_tokens=SEQ, name="mm_in_a")
    u_c = _matmul(h_c, [(w_in_a, 0, 0)], d, tm=BATCH * CTX_LEN, tn=512, epilogue="plain", out_dtype=F32,
                  chunk_tokens=CTX_LEN, single_buffer_lhs=True, name="mm_in_a_ctx")
    ops_f = _s5_prep(lam_re[0, 0], lam_im[0, 0], log_step[0, 0], b_re[0, 0], b_im[0, 0],
                     c_re[0, 0], c_im[0, 0], reverse=False)
    ops_b = _s5_prep(lam_re[0, 1], lam_im[0, 1], log_step[0, 1], b_re[0, 1], b_im[0, 1],
                     c_re[0, 1], c_im[0, 1], reverse=True)
    y_f = _s5_scan(u_l, u_c, ops_f, reverse=False, d_skip=d_skip)
    z = _s5_scan(u_l, u_c, ops_b, reverse=True, y_prev=y_f)
    gz = _matmul(z, [(w_glu_a, 0, 0)], d, tm=1024, tn=512, epilogue="glu", out_dtype=BF16,
                 chunked_lhs=True, single_buffer_lhs=True, name="mm_glu")
    y = _matmul_resid(gz, w_out_a, 0, x.reshape(m, d), mods[0], 2, tm=2048, tn=256, name="mm_out_a")

    h, mu, rs = _norm(y, ln_g, ln_b, 0, 0, mods[0], 3, 4)
    n_steps = (m // 2048) * (d_ff // 256)
    blk16 = d // 16
    side = [(w2.reshape(DEPTH * d_ff, d), DEPTH * d_ff // n_steps, 0, n_steps),
            (w_in_b.reshape(d, 3 * d), 16, 0, blk16),
            (w_out_b.reshape(d, d), 16, 0, blk16),
            (w1.reshape(DEPTH * d, d_ff), 16, blk16, blk16),
            (w3.reshape(DEPTH * d, d_ff), 16, blk16, blk16)]
    g, w2_b, w_in_b_b, w_out_b_b, w1_b, w3_b = _matmul(
        h, [(w1, 0, 0), (w3, 0, 0)], d_ff, tm=2048, tn=256, epilogue="swiglu", out_dtype=BF16,
        single_buffer_lhs=True, side_casts=side, name="mm_swiglu")
    w2_b = w2_b.reshape(DEPTH, d_ff, d)
    w_in_b_b = w_in_b_b.reshape(1, d, 3 * d)
    w_out_b_b = w_out_b_b.reshape(1, d, d)
    w1_b = w1_b.reshape(1, d, d_ff)
    w3_b = w3_b.reshape(1, d, d_ff)
    y = w2_resid(g, w2_b, 0, (y, mu, rs))

    h, mu, rs = _norm(y, ln_g, ln_b, 0, 1, mods[1], 0, 1)
    cw_spec = pl.BlockSpec((None, 3, 256), lambda i, j: (0, 0, j))
    gz = _matmul(h, [(w_in_b_b, 0, 0), (w_in_b_b, 0, d), (w_in_b_b, 0, 2 * d)], d, tm=1024, tn=256,
                 epilogue="conv", out_dtype=BF16, extra=(conv_w,), extra_specs=(cw_spec,), name="mm_conv")
    y = _matmul_resid(gz, w_out_b_b, 0, (y, mu, rs), mods[1], 2, tm=2048, tn=256, prev_ln=(0, 1),
                      ln_g=ln_g, ln_b=ln_b, name="mm_out_b")
    h, mu, rs = _norm(y, ln_g, ln_b, 1, 0, mods[1], 3, 4)
    g = _matmul(h, [(w1_b, 0, 0), (w3_b, 0, 0)], d_ff, tm=2048, tn=256, epilogue="swiglu", out_dtype=BF16,
                name="mm_swiglu")
    y = w2_resid(g, w2_b, 1, (y, mu, rs))
    return _norm(y, ln_g, ln_b, 1, 1).reshape(BATCH, SEQ, d)
```

```python
import functools

import jax
import jax.numpy as jnp
from jax import lax
from jax.experimental import pallas as pl
from jax.experimental.pallas import tpu as pltpu

F32 = jnp.float32
BF16 = jnp.bfloat16

D_MODEL = 4096
BATCH = 8
SEQ = 2048
CTX_LEN = 256
GROUP = 16
STATE = 64
GRID_W = 64
DEPTH = 2
LN_EPS = 1e-5
ALPHA = (2.0 * DEPTH) ** 0.25
MOD_ROWS = 16
N_MOD = 6

LANES = 128
SUBLANES = 8
VMEM_CEILING_BYTES = 60 * 1024 * 1024

T_CHUNK = 16
N_SLAB = D_MODEL // LANES
G_SLAB = LANES // GROUP
SP = G_SLAB * STATE
J_LAT = SEQ // T_CHUNK
J_CTX = CTX_LEN // T_CHUNK
JB = 64
N_JB = J_LAT // JB
KT = T_CHUNK * LANES
POW_ROWS = 32
NORM_ROWS = 16
NORM_UNROLL = 8


def _cparams(sem, nbytes):
    limit = int(min(VMEM_CEILING_BYTES, max(32 * 1024 * 1024, nbytes)))
    return pltpu.CompilerParams(dimension_semantics=sem, vmem_limit_bytes=limit)


def _mod_spec(k, grid_rank):
    if grid_rank == 1:
        return pl.BlockSpec((None, MOD_ROWS, D_MODEL), lambda i: (k, 0, 0))
    return pl.BlockSpec((None, MOD_ROWS, D_MODEL), lambda i, j: (k, 0, 0))


def _mod_kernel(c_ref, w_ref, b_ref, o_ref):
    c = c_ref[...]
    s = (c * jax.nn.sigmoid(c)).astype(BF16)
    o_ref[...] = jnp.dot(s, w_ref[...].astype(BF16), preferred_element_type=F32) + b_ref[...]


def _mod_rows(c_rows, w_mod, b_mod, layer):
    m, k = c_rows.shape
    n = w_mod.shape[2]
    tn = 512
    per = D_MODEL // tn
    return pl.pallas_call(
        _mod_kernel,
        grid=(n // tn,),
        in_specs=[pl.BlockSpec((m, k), lambda j: (0, 0)),
                  pl.BlockSpec((None, k, tn), lambda j: (layer, 0, j)),
                  pl.BlockSpec((None, 1, tn), lambda j: (layer, 0, j))],
        out_specs=pl.BlockSpec((None, m, tn), lambda j: (j // per, 0, j % per)),
        out_shape=jax.ShapeDtypeStruct((N_MOD, m, D_MODEL), F32),
        compiler_params=_cparams(("arbitrary",), 3 * k * tn * 4),
        name="mod_rows",
    )(c_rows, w_mod, b_mod.reshape(DEPTH, 1, n))


def _modulate_kernel(x_ref, sh_ref, sc_ref, o_ref, *, per, fixed_row):
    row = fixed_row if fixed_row is not None else pl.program_id(0) // per
    sh = sh_ref[pl.ds(row, 1), :]
    sc = sc_ref[pl.ds(row, 1), :]
    o_ref[...] = (x_ref[...] * (1.0 + sc) + sh).astype(o_ref.dtype)


def _modulate(x, mod, k_shift, k_scale, fixed_row=None, tr=512):
    b, n, d = x.shape
    tr = min(tr, n)
    per = n // tr
    return pl.pallas_call(
        functools.partial(_modulate_kernel, per=per, fixed_row=fixed_row),
        grid=(b * per,),
        in_specs=[pl.BlockSpec((None, tr, d), lambda i: (i // per, i % per, 0)),
                  _mod_spec(k_shift, 1), _mod_spec(k_scale, 1)],
        out_specs=pl.BlockSpec((tr, d), lambda i: (i, 0)),
        out_shape=jax.ShapeDtypeStruct((b * n, d), BF16),
        compiler_params=_cparams(("arbitrary",), 2 * tr * d * 6 + (8 << 20)),
        name="modulate",
    )(x, mod, mod)


def _norm_kernel(*refs, with_mod, per):
    if with_mod:
        y_ref, lg_ref, lb_ref, sh_ref, sc_ref, o_ref, mu_ref, rs_ref = refs
    else:
        y_ref, lg_ref, lb_ref, o_ref = refs
    lg = lg_ref[...]
    lb = lb_ref[...]
    if with_mod:
        b = pl.program_id(0) // per
        sc1 = 1.0 + sc_ref[pl.ds(b, 1), :]
        sh = sh_ref[pl.ds(b, 1), :]

    def rows(c, carry):
        r = pl.ds(pl.multiple_of(c * NORM_ROWS, NORM_ROWS), NORM_ROWS)
        y = y_ref[r, :]
        mu = jnp.mean(y, axis=-1, keepdims=True)
        yc = y - mu
        rs = lax.rsqrt(jnp.mean(yc * yc, axis=-1, keepdims=True) + LN_EPS)
        xn = yc * rs * lg + lb
        if with_mod:
            xn = xn * sc1 + sh
            mu_ref[r, :] = jnp.broadcast_to(mu, (NORM_ROWS, LANES))
            rs_ref[r, :] = jnp.broadcast_to(rs, (NORM_ROWS, LANES))
        o_ref[r, :] = xn.astype(o_ref.dtype)
        return carry

    lax.fori_loop(0, y_ref.shape[0] // NORM_ROWS, rows, 0, unroll=NORM_UNROLL)


def _ln_vec_spec(layer, sub, width, grid_rank):
    if grid_rank == 1:
        return pl.BlockSpec((None, None, 1, width), lambda i: (layer, sub, 0, 0))
    return pl.BlockSpec((None, None, 1, width), lambda i, j: (layer, sub, 0, j))


def _norm(y, ln_g, ln_b, layer, sub, mod=None, k_shift=None, k_scale=None, tr=512):
    m, d = y.shape
    per = SEQ // tr
    with_mod = mod is not None
    row = pl.BlockSpec((tr, d), lambda i: (i, 0))
    stat = pl.BlockSpec((tr, LANES), lambda i: (i, 0))
    vec = _ln_vec_spec(layer, sub, d, 1)
    in_specs = [row, vec, vec]
    args = [y, ln_g.reshape(DEPTH, 2, 1, d), ln_b.reshape(DEPTH, 2, 1, d)]
    if with_mod:
        in_specs += [_mod_spec(k_shift, 1), _mod_spec(k_scale, 1)]
        args += [mod, mod]
        out_specs = [row, stat, stat]
        out_shape = [jax.ShapeDtypeStruct((m, d), BF16), jax.ShapeDtypeStruct((m, LANES), F32),
                     jax.ShapeDtypeStruct((m, LANES), F32)]
    else:
        out_specs = row
        out_shape = jax.ShapeDtypeStruct((m, d), F32)
    return pl.pallas_call(
        functools.partial(_norm_kernel, with_mod=with_mod, per=per),
        grid=(m // tr,),
        in_specs=in_specs, out_specs=out_specs, out_shape=out_shape,
        compiler_params=_cparams(("arbitrary",), 2 * tr * d * 12 + (8 << 20)),
        name="norm_mod" if with_mod else "norm_out",
    )(*args)


def _mm_resid_kernel(*refs, has_stats, per, n_k):
    lhs_refs, w_refs, refs = refs[:n_k], refs[n_k:2 * n_k], refs[2 * n_k:]
    if has_stats:
        prev_ref, mu_ref, rs_ref, lg_ref, lb_ref, g_ref, y_ref = refs
    else:
        prev_ref, g_ref, y_ref = refs
    acc = jnp.dot(lhs_refs[0][...], w_refs[0][...].astype(BF16), preferred_element_type=F32)
    for lhs_ref, w_ref in zip(lhs_refs[1:], w_refs[1:]):
        acc = acc + jnp.dot(lhs_ref[...], w_ref[...].astype(BF16), preferred_element_type=F32)
    prev = prev_ref[...]
    if has_stats:
        prev = (prev - mu_ref[:, 0:1]) * rs_ref[:, 0:1] * lg_ref[...] + lb_ref[...]
    y_ref[...] = ALPHA * prev + g_ref[pl.ds(pl.program_id(0) // per, 1), :] * acc


def _matmul_resid(lhs, w, layer, prev, gate_mod, k_gate, *, tm, tn, prev_ln=None, ln_g=None, ln_b=None,
                  split_lhs=False, name="mm_resid"):
    m, k = lhs.shape
    d = D_MODEL
    per = SEQ // tm
    has_stats = prev_ln is not None
    tile = pl.BlockSpec((tm, tn), lambda i, j: (i, j))
    if split_lhs:
        n_k, kh = 2, k // 2
        in_specs = [pl.BlockSpec((tm, kh), lambda i, j: (i, 0)),
                    pl.BlockSpec((tm, kh), lambda i, j: (i, 1), pipeline_mode=pl.Buffered(1)),
                    pl.BlockSpec((None, kh, tn), lambda i, j: (layer, 0, j)),
                    pl.BlockSpec((None, kh, tn), lambda i, j: (layer, 1, j)), tile]
    else:
        n_k = 1
        in_specs = [pl.BlockSpec((tm, k), lambda i, j: (i, 0), pipeline_mode=pl.Buffered(1)),
                    pl.BlockSpec((None, k, tn), lambda i, j: (layer, 0, j)), tile]
    args = [lhs] * n_k + [w] * n_k
    if has_stats:
        y_prev, mu_prev, rs_prev = prev
        vec = _ln_vec_spec(prev_ln[0], prev_ln[1], tn, 2)
        stat_in = pl.BlockSpec((tm, LANES), lambda i, j: (i, 0), pipeline_mode=pl.Buffered(1))
        in_specs += [stat_in, stat_in, vec, vec]
        args += [y_prev, mu_prev, rs_prev, ln_g.reshape(DEPTH, 2, 1, d), ln_b.reshape(DEPTH, 2, 1, d)]
    else:
        args.append(prev)
    in_specs.append(pl.BlockSpec((None, MOD_ROWS, tn), lambda i, j: (k_gate, 0, j)))
    args.append(gate_mod)
    nbytes = (tm * k * lhs.dtype.itemsize * (n_k + 1) // 2 + k * tn * (2 * w.dtype.itemsize + 2)
              + tm * tn * 4 * 8 + tm * LANES * 4 * 2 + (4 << 20))
    return pl.pallas_call(
        functools.partial(_mm_resid_kernel, has_stats=has_stats, per=per, n_k=n_k),
        grid=(m // tm, d // tn),
        in_specs=in_specs,
        out_specs=tile,
        out_shape=jax.ShapeDtypeStruct((m, d), F32),
        compiler_params=_cparams(("arbitrary", "arbitrary"), nbytes),
        name=name,
    )(*args)


def _mm_kernel(*refs, n_w, epilogue, chunked_lhs, out_batches, tn, n_side, n_k):
    it = iter(refs)
    lhs_refs = [next(it) for _ in range(n_k)]
    w_refs = [[next(it) for _ in range(n_k)] for _ in range(n_w)]
    extra = [next(it) for _ in range(1 if epilogue == "conv" else 0)]
    side_src = [next(it) for _ in range(n_side)]
    o_ref = next(it)
    side_dst = [next(it) for _ in range(n_side)]
    scratch = list(it)
    lhs_ref = lhs_refs[0]
    j = pl.program_id(1)
    for src, dst in zip(side_src, side_dst):
        dst[...] = src[...].astype(BF16)
    if chunked_lhs:
        lhs_b_ref, = scratch
        tm, k = lhs_b_ref.shape

        @pl.when(j == 0)
        def _cast():
            lhs_b_ref[...] = lhs_ref[...].reshape(tm, k).astype(BF16)

        lhs_parts = [lhs_b_ref[...]]
    else:
        lhs_parts = [r[...] for r in lhs_refs]
    accs = []
    for ws in w_refs:
        acc = jnp.dot(lhs_parts[0], ws[0][...].astype(BF16), preferred_element_type=F32)
        for part, w in zip(lhs_parts[1:], ws[1:]):
            acc = acc + jnp.dot(part, w[...].astype(BF16), preferred_element_type=F32)
        accs.append(acc)
    if epilogue == "plain":
        out = accs[0]
    elif epilogue == "glu":
        zt = lhs_ref[:, :, pl.ds(pl.multiple_of(j * tn, tn), tn)]
        out = zt.reshape(accs[0].shape) * jax.nn.sigmoid(accs[0])
    elif epilogue == "swiglu":
        a1, a3 = accs
        out = a1 * jax.nn.sigmoid(a1) * a3
    elif epilogue == "conv":
        gb, gc, v = accs
        cw_ref, = extra
        t = gc * v
        tm = t.shape[0]
        pos = lax.broadcasted_iota(jnp.int32, t.shape, 0) % GRID_W
        prev = jnp.where(pos == 0, 0.0, pltpu.roll(t, 1, 0))
        nxt = jnp.where(pos == GRID_W - 1, 0.0, pltpu.roll(t, tm - 1, 0))
        out = gb * (cw_ref[0:1, :] * prev + cw_ref[1:2, :] * t + cw_ref[2:3, :] * nxt)
    else:
        raise ValueError(epilogue)
    out = out.astype(o_ref.dtype)
    if out_batches is None:
        o_ref[...] = out
    elif out_batches == 1:
        o_ref[...] = out.reshape(o_ref.shape)
    else:
        n_tok = out.shape[0] // out_batches
        for bb in range(out_batches):
            o_ref[:, bb, :, :] = out[bb * n_tok:(bb + 1) * n_tok].reshape(n_tok // T_CHUNK, T_CHUNK, tn)


def _matmul(lhs, weights, n_out, *, tm, tn, epilogue, out_dtype, chunked_lhs=False, chunk_tokens=None,
            extra=(), extra_specs=(), single_buffer_lhs=False, split_lhs=False, side_casts=(), name="matmul"):
    k = weights[0][0].shape[1]
    buffered = dict(pipeline_mode=pl.Buffered(1)) if single_buffer_lhs else {}
    scratch_shapes = []
    n_k = 2 if split_lhs else 1
    kh = k // n_k
    if chunked_lhs:
        jn, bn, tt, _ = lhs.shape
        m = jn * bn * tt
        per = (jn * tt) // tm
        lhs_specs = [pl.BlockSpec((tm // tt, None, tt, k), lambda i, j: (i % per, i // per, 0, 0), **buffered)]
        scratch_shapes.append(pltpu.VMEM((tm, k), BF16))
    elif split_lhs:
        m = lhs.shape[0]
        lhs_specs = [pl.BlockSpec((tm, kh), lambda i, j: (i, 0)),
                     pl.BlockSpec((tm, kh), lambda i, j: (i, 1), pipeline_mode=pl.Buffered(1))]
    else:
        m = lhs.shape[0]
        lhs_specs = [pl.BlockSpec((tm, k), lambda i, j: (i, 0), **buffered)]
    w_specs = [pl.BlockSpec((None, kh, tn), functools.partial(lambda i, j, layer, blk, kk: (layer, kk, j + blk),
                                                              layer=layer, blk=off // tn, kk=kk))
               for (_, layer, off) in weights for kk in range(n_k)]
    out_batches = None
    if chunk_tokens is not None:
        jn = chunk_tokens // T_CHUNK
        if tm <= chunk_tokens:
            out_batches = 1
            per_o = chunk_tokens // tm
            out_spec = pl.BlockSpec((tm // T_CHUNK, None, T_CHUNK, tn), lambda i, j: (i % per_o, i // per_o, 0, j))
        else:
            out_batches = tm // chunk_tokens
            out_spec = pl.BlockSpec((jn, out_batches, T_CHUNK, tn), lambda i, j: (0, i, 0, j))
        out_shape = jax.ShapeDtypeStruct((jn, BATCH, T_CHUNK, n_out), out_dtype)
    else:
        out_spec = pl.BlockSpec((tm, tn), lambda i, j: (i, j))
        out_shape = jax.ShapeDtypeStruct((m, n_out), out_dtype)
    w_arrays = [w for (w, _, _) in weights for _ in range(n_k)]
    lhs_bufs = 1.5 if split_lhs else (1 if single_buffer_lhs else 2)
    lhs_bytes = int(tm * k * lhs.dtype.itemsize * lhs_bufs) + (tm * k * 2 if chunked_lhs else 0)
    w_bytes = sum(2 * k * tn * w.dtype.itemsize + (k * tn * 2 if w.dtype != BF16 else 0) for (w, _, _) in weights)
    out_bytes = tm * tn * (2 * jnp.dtype(out_dtype).itemsize + 4 * (len(weights) + 3))
    n_j = n_out // tn
    side_in_specs, side_out_specs, side_out_shapes, side_srcs, side_bytes = [], [], [], [], 0
    for src, rows, first, n_blocks in side_casts:
        cols = src.shape[1]

        def blk(i, j, n_blocks=n_blocks):
            return jnp.minimum(i * n_j + j, n_blocks - 1)

        side_in_specs.append(pl.BlockSpec((rows, cols), lambda i, j, blk=blk, first=first: (blk(i, j) + first, 0)))
        side_out_specs.append(pl.BlockSpec((rows, cols), lambda i, j, blk=blk: (blk(i, j), 0)))
        side_out_shapes.append(jax.ShapeDtypeStruct((rows * n_blocks, cols), BF16))
        side_srcs.append(src)
        side_bytes += 2 * rows * cols * 6
    res = pl.pallas_call(
        functools.partial(_mm_kernel, n_w=len(weights), epilogue=epilogue, chunked_lhs=chunked_lhs,
                          out_batches=out_batches, tn=tn, n_side=len(side_srcs), n_k=n_k),
        grid=(m // tm, n_j),
        in_specs=lhs_specs + w_specs + list(extra_specs) + side_in_specs,
        out_specs=[out_spec] + side_out_specs, out_shape=[out_shape] + side_out_shapes,
        scratch_shapes=scratch_shapes,
        compiler_params=_cparams(("arbitrary", "arbitrary"),
                                 lhs_bytes + w_bytes + out_bytes + side_bytes + (4 << 20)),
        name=name,
    )(*([lhs] * n_k), *w_arrays, *extra, *side_srcs)
    return res if side_srcs else res[0]


def _s5_prep_kernel(lr_ref, li_ref, ls_ref, bre_ref, bim_ref, cre_ref, cim_ref,
                    tall_ref, w_ref, v_ref, at_ref, *, reverse):
    lr = lr_ref[...]
    li = li_ref[...]
    dt = jnp.exp(ls_ref[...])
    kk = jnp.minimum(lax.broadcasted_iota(jnp.int32, (POW_ROWS, SP), 0), T_CHUNK).astype(F32)
    mag = jnp.exp(kk * (lr * dt))
    ang = kk * (li * dt)
    p_re = mag * jnp.cos(ang)
    p_im = mag * jnp.sin(ang)
    a_re = p_re[1:2, :]
    a_im = p_im[1:2, :]
    n_re, n_im = a_re - 1.0, a_im
    den = lr * lr + li * li
    f_re = (n_re * lr + n_im * li) / den
    f_im = (n_im * lr - n_re * li) / den
    sel_r = lax.broadcasted_iota(jnp.int32, (STATE, SP), 0)
    sel_c = lax.broadcasted_iota(jnp.int32, (STATE, SP), 1)
    spread = ((sel_c & (STATE - 1)) == sel_r).astype(F32)
    blk_r = lax.broadcasted_iota(jnp.int32, (LANES, SP), 0) // GROUP
    blk_c = lax.broadcasted_iota(jnp.int32, (LANES, SP), 1) // STATE
    on_diag = blk_r == blk_c

    def blockdiag(ref):
        full = jnp.dot(ref[...], spread, preferred_element_type=F32, precision=lax.Precision.HIGHEST)
        return jnp.where(on_diag, full, 0.0)

    b_re = blockdiag(bre_ref)
    b_im = blockdiag(bim_ref)
    bb_re = f_re * b_re - f_im * b_im
    bb_im = f_re * b_im + f_im * b_re
    c_re = blockdiag(cre_ref).T
    c_im = blockdiag(cim_ref).T
    c_re_b = c_re.astype(BF16)
    c_im_b = c_im.astype(BF16)

    dks = []
    for k in range(T_CHUNK):
        pr, pi = p_re[k:k + 1, :], p_im[k:k + 1, :]
        bk_re = (bb_re * pr - bb_im * pi).astype(BF16)
        bk_im = (bb_re * pi + bb_im * pr).astype(BF16)
        dk = (jnp.dot(bk_re, c_re_b, preferred_element_type=F32)
              - jnp.dot(bk_im, c_im_b, preferred_element_type=F32))
        dks.append(dk.astype(BF16))
        m = k if reverse else T_CHUNK - 1 - k
        w_ref[m * LANES:(m + 1) * LANES, 0:SP] = bk_re
        w_ref[m * LANES:(m + 1) * LANES, SP:2 * SP] = bk_im

    zero = jnp.zeros((LANES, LANES), BF16)
    for r in range(T_CHUNK):
        for c in range(2):
            idx = (r - c) if reverse else (T_CHUNK - 2 - r + c)
            blk = dks[idx] if 0 <= idx < T_CHUNK else zero
            tall_ref[r * LANES:(r + 1) * LANES, c * LANES:(c + 1) * LANES] = blk

    pad = jnp.zeros((LANES - POW_ROWS, SP), F32)
    pt_re = jnp.concatenate([p_re, pad], axis=0).T
    pt_im = jnp.concatenate([p_im, pad], axis=0).T
    for t in range(T_CHUNK):
        k = (T_CHUNK - t) if reverse else (t + 1)
        pr, pi = pt_re[:, k:k + 1], pt_im[:, k:k + 1]
        v_ref[0:SP, t * LANES:(t + 1) * LANES] = (pr * c_re - pi * c_im).astype(BF16)
        v_ref[SP:2 * SP, t * LANES:(t + 1) * LANES] = (-(pr * c_im + pi * c_re)).astype(BF16)

    at_ref[:, 0:SP] = p_re[T_CHUNK:T_CHUNK + 1, :]
    at_ref[:, SP:2 * SP] = p_im[T_CHUNK:T_CHUNK + 1, :]


def _s5_prep(lam_re, lam_im, log_step, b_re, b_im, c_re, c_im, *, reverse):
    def rows(v):
        return v.reshape(N_SLAB, 1, SP)

    def b_compact(b):
        return b.reshape(N_SLAB, G_SLAB, STATE, GROUP).transpose(0, 1, 3, 2).reshape(N_SLAB, LANES, STATE)

    def c_compact(c):
        return c.reshape(N_SLAB, LANES, STATE)

    ls = jnp.repeat(log_step, STATE).reshape(N_SLAB, 1, SP)
    vec = pl.BlockSpec((None, 1, SP), lambda s: (s, 0, 0))
    cmp_spec = pl.BlockSpec((None, LANES, STATE), lambda s: (s, 0, 0))
    return pl.pallas_call(
        functools.partial(_s5_prep_kernel, reverse=reverse),
        grid=(N_SLAB,),
        in_specs=[vec, vec, vec, cmp_spec, cmp_spec, cmp_spec, cmp_spec],
        out_specs=[pl.BlockSpec((None, KT, 2 * LANES), lambda s: (s, 0, 0)),
                   pl.BlockSpec((None, KT, 2 * SP), lambda s: (s, 0, 0)),
                   pl.BlockSpec((None, 2 * SP, KT), lambda s: (s, 0, 0)),
                   pl.BlockSpec((None, 1, 2 * SP), lambda s: (s, 0, 0))],
        out_shape=[jax.ShapeDtypeStruct((N_SLAB, KT, 2 * LANES), BF16),
                   jax.ShapeDtypeStruct((N_SLAB, KT, 2 * SP), BF16),
                   jax.ShapeDtypeStruct((N_SLAB, 2 * SP, KT), BF16),
                   jax.ShapeDtypeStruct((N_SLAB, 1, 2 * SP), F32)],
        compiler_params=_cparams(("arbitrary",), 40 << 20),
        name="s5_prep_bwd" if reverse else "s5_prep_fwd",
    )(rows(lam_re), rows(lam_im), ls, b_compact(b_re), b_compact(b_im), c_compact(c_re), c_compact(c_im))


def _s5_scan_kernel(*refs, reverse, second):
    it = iter(refs)
    ul_ref = next(it)
    uc_ref = next(it)
    y1_ref = next(it) if second else None
    tall_ref, w_ref, v_ref, at_ref = next(it), next(it), next(it), next(it)
    dsk_ref = None if second else next(it)
    o_ref = next(it)
    x2_ref, xc2_ref, xst_ref, sprev_ref, state_ref = it

    rows = JB * BATCH
    crow = J_CTX * BATCH
    a_re = at_ref[:, 0:SP]
    a_im = at_ref[:, SP:2 * SP]

    def token(ref, t, n):
        return ref[pl.ds(t, n, stride=T_CHUNK), :]

    def scan(n_chunks, keep):
        s_re = state_ref[:, 0:SP]
        s_im = state_ref[:, SP:2 * SP]
        order = range(n_chunks - 1, -1, -1) if reverse else range(n_chunks)
        for jj in order:
            r = slice(jj * BATCH, (jj + 1) * BATCH)
            if keep:
                sprev_ref[r, 0:SP] = s_re
                sprev_ref[r, SP:2 * SP] = s_im
            x_re = xst_ref[r, 0:SP]
            x_im = xst_ref[r, SP:2 * SP]
            s_re, s_im = (a_re * s_re - a_im * s_im + x_re,
                          a_re * s_im + a_im * s_re + x_im)
        state_ref[:, 0:SP] = s_re
        state_ref[:, SP:2 * SP] = s_im

    @pl.when(pl.program_id(1) == 0)
    def _context():
        state_ref[...] = jnp.zeros_like(state_ref)
        for t in range(T_CHUNK):
            xc2_ref[:, t * LANES:(t + 1) * LANES] = token(uc_ref, t, crow).astype(BF16)
        xst_ref[0:crow, :] = jnp.dot(xc2_ref[...], w_ref[...], preferred_element_type=F32)
        scan(J_CTX, keep=False)

    for t in range(T_CHUNK):
        x2_ref[:, t * LANES:(t + 1) * LANES] = token(ul_ref, t, rows).astype(BF16)
    xst_ref[...] = jnp.dot(x2_ref[...], w_ref[...], preferred_element_type=F32)
    scan(JB, keep=True)
    sp = sprev_ref[...].astype(BF16)

    n_tiles = KT // (2 * LANES)
    for nt in range(n_tiles):
        cols = slice(nt * 2 * LANES, (nt + 1) * 2 * LANES)
        if reverse:
            xa = x2_ref[:, nt * 2 * LANES:]
            ta = tall_ref[0:(n_tiles - nt) * 2 * LANES, :]
        else:
            xa = x2_ref[:, 0:(nt + 1) * 2 * LANES]
            ta = tall_ref[(n_tiles - 1 - nt) * 2 * LANES:, :]
        y = (jnp.dot(xa, ta, preferred_element_type=F32)
             + jnp.dot(sp, v_ref[:, cols], preferred_element_type=F32))
        for c in range(2):
            t = 2 * nt + c
            yt = y[:, c * LANES:(c + 1) * LANES]
            if second:
                val = jax.nn.gelu(token(y1_ref, t, rows) + yt)
            else:
                val = dsk_ref[...] * token(ul_ref, t, rows) + yt
            o_ref[pl.ds(t, rows, stride=T_CHUNK), :] = val


def _s5_scan(u_lat, u_ctx, ops, *, reverse, y_prev=None, d_skip=None):
    tall, w, v, at = ops
    second = y_prev is not None
    blk_rows = JB * BATCH * T_CHUNK
    ctx_rows = J_CTX * BATCH * T_CHUNK

    def jblk(q):
        return (N_JB - 1 - q) if reverse else q

    lat_spec = pl.BlockSpec((blk_rows, LANES), lambda s, q: (jblk(q), s))
    in_specs = [lat_spec, pl.BlockSpec((ctx_rows, LANES), lambda s, q: (0, s))]
    args = [u_lat.reshape(J_LAT * BATCH * T_CHUNK, D_MODEL), u_ctx.reshape(ctx_rows, D_MODEL)]
    if second:
        in_specs.append(lat_spec)
        args.append(y_prev.reshape(J_LAT * BATCH * T_CHUNK, D_MODEL))
    in_specs += [pl.BlockSpec((None, KT, 2 * LANES), lambda s, q: (s, 0, 0)),
                 pl.BlockSpec((None, KT, 2 * SP), lambda s, q: (s, 0, 0)),
                 pl.BlockSpec((None, 2 * SP, KT), lambda s, q: (s, 0, 0)),
                 pl.BlockSpec((None, 1, 2 * SP), lambda s, q: (s, 0, 0))]
    args += [tall, w, v, at]
    if not second:
        in_specs.append(pl.BlockSpec((None, None, 1, LANES), lambda s, q: (0, s, 0, 0)))
        args.append(d_skip.reshape(1, N_SLAB, 1, LANES))
    rows = JB * BATCH
    out = pl.pallas_call(
        functools.partial(_s5_scan_kernel, reverse=reverse, second=second),
        grid=(N_SLAB, N_JB),
        in_specs=in_specs,
        out_specs=lat_spec,
        out_shape=jax.ShapeDtypeStruct((J_LAT * BATCH * T_CHUNK, D_MODEL), F32),
        scratch_shapes=[pltpu.VMEM((rows, KT), BF16),
                        pltpu.VMEM((J_CTX * BATCH, KT), BF16),
                        pltpu.VMEM((rows, 2 * SP), F32),
                        pltpu.VMEM((rows, 2 * SP), F32),
                        pltpu.VMEM((BATCH, 2 * SP), F32)],
        compiler_params=_cparams(("arbitrary", "arbitrary"), VMEM_CEILING_BYTES),
        name="s5_scan_bwd" if reverse else "s5_scan_fwd",
    )(*args)
    return out.reshape(J_LAT, BATCH, T_CHUNK, D_MODEL)


def kernel(x, c, ctx, c_ctx, w_mod, b_mod, ln_g, ln_b, w_in_a, lam_re, lam_im, log_step,
           b_re, b_im, c_re, c_im, d_skip, w_glu_a, w_out_a, w_in_b, conv_w, w_out_b, w1, w3, w2):
    d = D_MODEL
    d_ff = w1.shape[-1]
    m = BATCH * SEQ

    c_rows = jnp.concatenate([c, c_ctx[None, :], jnp.zeros((MOD_ROWS - BATCH - 1, d), F32)], axis=0)
    mods = [_mod_rows(c_rows, w_mod, b_mod, i) for i in range(DEPTH)]

    def w2_resid(g, w2_any, layer, stream):
        return _matmul_resid(g, w2_any, layer, stream, mods[layer], 5, tm=1024, tn=512, prev_ln=(layer, 0),
                             ln_g=ln_g, ln_b=ln_b, name="mm_w2")

    h_l = _modulate(x, mods[0], 0, 1)
    h_c = _modulate(ctx, mods[0], 0, 1, fixed_row=BATCH)
    u_l = _matmul(h_l, [(w_in_a, 0, 0)], d, tm=1024, tn=512, epilogue="plain", out_dtype=F32,
                  chunk_tokens=SEQ, name="mm_in_a")
    u_c = _matmul(h_c, [(w_in_a, 0, 0)], d, tm=BATCH * CTX_LEN, tn=512, epilogue="plain", out_dtype=F32,
                  chunk_tokens=CTX_LEN, single_buffer_lhs=True, name="mm_in_a_ctx")
    ops_f = _s5_prep(lam_re[0, 0], lam_im[0, 0], log_step[0, 0], b_re[0, 0], b_im[0, 0],
                     c_re[0, 0], c_im[0, 0], reverse=False)
    ops_b = _s5_prep(lam_re[0, 1], lam_im[0, 1], log_step[0, 1], b_re[0, 1], b_im[0, 1],
                     c_re[0, 1], c_im[0, 1], reverse=True)
    y_f = _s5_scan(u_l, u_c, ops_f, reverse=False, d_skip=d_skip)
    z = _s5_scan(u_l, u_c, ops_b, reverse=True, y_prev=y_f)
    gz = _matmul(z, [(w_glu_a, 0, 0)], d, tm=1024, tn=512, epilogue="glu", out_dtype=BF16,
                 chunked_lhs=True, single_buffer_lhs=True, name="mm_glu")
    y = _matmul_resid(gz, w_out_a, 0, x.reshape(m, d), mods[0], 2, tm=2048, tn=256, split_lhs=True,
                      name="mm_out_a")

    h, mu, rs = _norm(y, ln_g, ln_b, 0, 0, mods[0], 3, 4)
    n_steps = (m // 2048) * (d_ff // 256)
    blk16 = d // 16
    side = [(w2.reshape(DEPTH * d_ff, d), DEPTH * d_ff // n_steps, 0, n_steps),
            (w_in_b.reshape(d, 3 * d), 16, 0, blk16),
            (w_out_b.reshape(d, d), 16, 0, blk16),
            (w1.reshape(DEPTH * d, d_ff), 16, blk16, blk16),
            (w3.reshape(DEPTH * d, d_ff), 16, blk16, blk16)]
    g, w2_b, w_in_b_b, w_out_b_b, w1_b, w3_b = _matmul(
        h, [(w1, 0, 0), (w3, 0, 0)], d_ff, tm=2048, tn=256, epilogue="swiglu", out_dtype=BF16,
        split_lhs=True, side_casts=side, name="mm_swiglu")
    w2_b = w2_b.reshape(DEPTH, d_ff, d)
    w_in_b_b = w_in_b_b.reshape(1, d, 3 * d)
    w_out_b_b = w_out_b_b.reshape(1, d, d)
    w1_b = w1_b.reshape(1, d, d_ff)
    w3_b = w3_b.reshape(1, d, d_ff)
    y = w2_resid(g, w2_b, 0, (y, mu, rs))

    h, mu, rs = _norm(y, ln_g, ln_b, 0, 1, mods[1], 0, 1)
    cw_spec = pl.BlockSpec((None, 3, 256), lambda i, j: (0, 0, j))
    gz = _matmul(h, [(w_in_b_b, 0, 0), (w_in_b_b, 0, d), (w_in_b_b, 0, 2 * d)], d, tm=1024, tn=256,
                 epilogue="conv", out_dtype=BF16, extra=(conv_w,), extra_specs=(cw_spec,), name="mm_conv")
    y = _matmul_resid(gz, w_out_b_b, 0, (y, mu, rs), mods[1], 2, tm=2048, tn=256, prev_ln=(0, 1),
                      ln_g=ln_g, ln_b=ln_b, split_lhs=True, name="mm_out_b")
    h, mu, rs = _norm(y, ln_g, ln_b, 1, 0, mods[1], 3, 4)
    g = _matmul(h, [(w1_b, 0, 0), (w3_b, 0, 0)], d_ff, tm=2048, tn=256, epilogue="swiglu", out_dtype=BF16,
                name="mm_swiglu")
    y = w2_resid(g, w2_b, 1, (y, mu, rs))
    return _norm(y, ln_g, ln_b, 1, 1).reshape(BATCH, SEQ, d)
```

```python
import functools

import jax
import jax.numpy as jnp
from jax import lax
from jax.experimental import pallas as pl
from jax.experimental.pallas import tpu as pltpu

F32 = jnp.float32
BF16 = jnp.bfloat16

D_MODEL = 4096
BATCH = 8
SEQ = 2048
CTX_LEN = 256
GROUP = 16
STATE = 64
GRID_W = 64
DEPTH = 2
LN_EPS = 1e-5
ALPHA = (2.0 * DEPTH) ** 0.25
MOD_ROWS = 16
N_MOD = 6

LANES = 128
SUBLANES = 8
VMEM_CEILING_BYTES = 60 * 1024 * 1024

T_CHUNK = 16
N_SLAB = D_MODEL // LANES
G_SLAB = LANES // GROUP
SP = G_SLAB * STATE
J_LAT = SEQ // T_CHUNK
J_CTX = CTX_LEN // T_CHUNK
JB = 64
N_JB = J_LAT // JB
KT = T_CHUNK * LANES
POW_ROWS = 32
NORM_ROWS = 16
NORM_UNROLL = 8


def _cparams(sem, nbytes):
    limit = int(min(VMEM_CEILING_BYTES, max(32 * 1024 * 1024, nbytes)))
    return pltpu.CompilerParams(dimension_semantics=sem, vmem_limit_bytes=limit)


def _mod_spec(k, grid_rank):
    if grid_rank == 1:
        return pl.BlockSpec((None, MOD_ROWS, D_MODEL), lambda i: (k, 0, 0))
    return pl.BlockSpec((None, MOD_ROWS, D_MODEL), lambda i, j: (k, 0, 0))


def _mod_kernel(c_ref, w_ref, b_ref, o_ref):
    c = c_ref[...]
    s = (c * jax.nn.sigmoid(c)).astype(BF16)
    o_ref[...] = jnp.dot(s, w_ref[...].astype(BF16), preferred_element_type=F32) + b_ref[...]


def _mod_rows(c_rows, w_mod, b_mod, layer):
    m, k = c_rows.shape
    n = w_mod.shape[2]
    tn = 512
    per = D_MODEL // tn
    return pl.pallas_call(
        _mod_kernel,
        grid=(n // tn,),
        in_specs=[pl.BlockSpec((m, k), lambda j: (0, 0)),
                  pl.BlockSpec((None, k, tn), lambda j: (layer, 0, j)),
                  pl.BlockSpec((None, 1, tn), lambda j: (layer, 0, j))],
        out_specs=pl.BlockSpec((None, m, tn), lambda j: (j // per, 0, j % per)),
        out_shape=jax.ShapeDtypeStruct((N_MOD, m, D_MODEL), F32),
        compiler_params=_cparams(("arbitrary",), 3 * k * tn * 4),
        name="mod_rows",
    )(c_rows, w_mod, b_mod.reshape(DEPTH, 1, n))


def _modulate_kernel(x_ref, sh_ref, sc_ref, o_ref, *, per, fixed_row):
    row = fixed_row if fixed_row is not None else pl.program_id(0) // per
    sh = sh_ref[pl.ds(row, 1), :]
    sc = sc_ref[pl.ds(row, 1), :]
    o_ref[...] = (x_ref[...] * (1.0 + sc) + sh).astype(o_ref.dtype)


def _modulate(x, mod, k_shift, k_scale, fixed_row=None, tr=512):
    b, n, d = x.shape
    tr = min(tr, n)
    per = n // tr
    return pl.pallas_call(
        functools.partial(_modulate_kernel, per=per, fixed_row=fixed_row),
        grid=(b * per,),
        in_specs=[pl.BlockSpec((None, tr, d), lambda i: (i // per, i % per, 0)),
                  _mod_spec(k_shift, 1), _mod_spec(k_scale, 1)],
        out_specs=pl.BlockSpec((tr, d), lambda i: (i, 0)),
        out_shape=jax.ShapeDtypeStruct((b * n, d), BF16),
        compiler_params=_cparams(("arbitrary",), 2 * tr * d * 6 + (8 << 20)),
        name="modulate",
    )(x, mod, mod)


def _norm_kernel(*refs, with_mod, per):
    if with_mod:
        y_ref, lg_ref, lb_ref, sh_ref, sc_ref, o_ref, mu_ref, rs_ref = refs
    else:
        y_ref, lg_ref, lb_ref, o_ref = refs
    lg = lg_ref[...]
    lb = lb_ref[...]
    if with_mod:
        b = pl.program_id(0) // per
        sc1 = 1.0 + sc_ref[pl.ds(b, 1), :]
        sh = sh_ref[pl.ds(b, 1), :]

    def rows(c, carry):
        r = pl.ds(pl.multiple_of(c * NORM_ROWS, NORM_ROWS), NORM_ROWS)
        y = y_ref[r, :]
        mu = jnp.mean(y, axis=-1, keepdims=True)
        yc = y - mu
        rs = lax.rsqrt(jnp.mean(yc * yc, axis=-1, keepdims=True) + LN_EPS)
        xn = yc * rs * lg + lb
        if with_mod:
            xn = xn * sc1 + sh
            mu_ref[r, :] = jnp.broadcast_to(mu, (NORM_ROWS, LANES))
            rs_ref[r, :] = jnp.broadcast_to(rs, (NORM_ROWS, LANES))
        o_ref[r, :] = xn.astype(o_ref.dtype)
        return carry

    lax.fori_loop(0, y_ref.shape[0] // NORM_ROWS, rows, 0, unroll=NORM_UNROLL)


def _ln_vec_spec(layer, sub, width, grid_rank):
    if grid_rank == 1:
        return pl.BlockSpec((None, None, 1, width), lambda i: (layer, sub, 0, 0))
    return pl.BlockSpec((None, None, 1, width), lambda i, j: (layer, sub, 0, j))


def _norm(y, ln_g, ln_b, layer, sub, mod=None, k_shift=None, k_scale=None, tr=512):
    m, d = y.shape
    per = SEQ // tr
    with_mod = mod is not None
    row = pl.BlockSpec((tr, d), lambda i: (i, 0))
    stat = pl.BlockSpec((tr, LANES), lambda i: (i, 0))
    vec = _ln_vec_spec(layer, sub, d, 1)
    in_specs = [row, vec, vec]
    args = [y, ln_g.reshape(DEPTH, 2, 1, d), ln_b.reshape(DEPTH, 2, 1, d)]
    if with_mod:
        in_specs += [_mod_spec(k_shift, 1), _mod_spec(k_scale, 1)]
        args += [mod, mod]
        out_specs = [row, stat, stat]
        out_shape = [jax.ShapeDtypeStruct((m, d), BF16), jax.ShapeDtypeStruct((m, LANES), F32),
                     jax.ShapeDtypeStruct((m, LANES), F32)]
    else:
        out_specs = row
        out_shape = jax.ShapeDtypeStruct((m, d), F32)
    return pl.pallas_call(
        functools.partial(_norm_kernel, with_mod=with_mod, per=per),
        grid=(m // tr,),
        in_specs=in_specs, out_specs=out_specs, out_shape=out_shape,
        compiler_params=_cparams(("arbitrary",), 2 * tr * d * 12 + (8 << 20)),
        name="norm_mod" if with_mod else "norm_out",
    )(*args)


def _mm_resid_kernel(*refs, has_stats, per, n_k):
    lhs_refs, w_refs, refs = refs[:n_k], refs[n_k:2 * n_k], refs[2 * n_k:]
    if has_stats:
        prev_ref, mu_ref, rs_ref, lg_ref, lb_ref, g_ref, y_ref = refs
    else:
        prev_ref, g_ref, y_ref = refs
    acc = jnp.dot(lhs_refs[0][...], w_refs[0][...].astype(BF16), preferred_element_type=F32)
    for lhs_ref, w_ref in zip(lhs_refs[1:], w_refs[1:]):
        acc = acc + jnp.dot(lhs_ref[...], w_ref[...].astype(BF16), preferred_element_type=F32)
    prev = prev_ref[...]
    if has_stats:
        prev = (prev - mu_ref[:, 0:1]) * rs_ref[:, 0:1] * lg_ref[...] + lb_ref[...]
    y_ref[...] = ALPHA * prev + g_ref[pl.ds(pl.program_id(0) // per, 1), :] * acc


def _matmul_resid(lhs, w, layer, prev, gate_mod, k_gate, *, tm, tn, prev_ln=None, ln_g=None, ln_b=None,
                  split_lhs=False, name="mm_resid"):
    m, k = lhs.shape
    d = D_MODEL
    per = SEQ // tm
    has_stats = prev_ln is not None
    tile = pl.BlockSpec((tm, tn), lambda i, j: (i, j))
    if split_lhs:
        n_k, kh = 2, k // 2
        in_specs = [pl.BlockSpec((tm, kh), lambda i, j: (i, 0)),
                    pl.BlockSpec((tm, kh), lambda i, j: (i, 1), pipeline_mode=pl.Buffered(1)),
                    pl.BlockSpec((None, kh, tn), lambda i, j: (layer, 0, j)),
                    pl.BlockSpec((None, kh, tn), lambda i, j: (layer, 1, j)), tile]
    else:
        n_k = 1
        in_specs = [pl.BlockSpec((tm, k), lambda i, j: (i, 0), pipeline_mode=pl.Buffered(1)),
                    pl.BlockSpec((None, k, tn), lambda i, j: (layer, 0, j)), tile]
    args = [lhs] * n_k + [w] * n_k
    if has_stats:
        y_prev, mu_prev, rs_prev = prev
        vec = _ln_vec_spec(prev_ln[0], prev_ln[1], tn, 2)
        stat_in = pl.BlockSpec((tm, LANES), lambda i, j: (i, 0), pipeline_mode=pl.Buffered(1))
        in_specs += [stat_in, stat_in, vec, vec]
        args += [y_prev, mu_prev, rs_prev, ln_g.reshape(DEPTH, 2, 1, d), ln_b.reshape(DEPTH, 2, 1, d)]
    else:
        args.append(prev)
    in_specs.append(pl.BlockSpec((None, MOD_ROWS, tn), lambda i, j: (k_gate, 0, j)))
    args.append(gate_mod)
    nbytes = (tm * k * lhs.dtype.itemsize * (n_k + 1) // 2 + k * tn * (2 * w.dtype.itemsize + 2)
              + tm * tn * 4 * 8 + tm * LANES * 4 * 2 + (4 << 20))
    return pl.pallas_call(
        functools.partial(_mm_resid_kernel, has_stats=has_stats, per=per, n_k=n_k),
        grid=(m // tm, d // tn),
        in_specs=in_specs,
        out_specs=tile,
        out_shape=jax.ShapeDtypeStruct((m, d), F32),
        compiler_params=_cparams(("arbitrary", "arbitrary"), nbytes),
        name=name,
    )(*args)


def _mm_kernel(*refs, n_w, epilogue, chunked_lhs, out_batches, tn, n_side, n_k):
    it = iter(refs)
    lhs_refs = [next(it) for _ in range(n_k)]
    w_refs = [[next(it) for _ in range(n_k)] for _ in range(n_w)]
    extra = [next(it) for _ in range(1 if epilogue == "conv" else 0)]
    side_src = [next(it) for _ in range(n_side)]
    o_ref = next(it)
    side_dst = [next(it) for _ in range(n_side)]
    scratch = list(it)
    lhs_ref = lhs_refs[0]
    j = pl.program_id(1)
    for src, dst in zip(side_src, side_dst):
        dst[...] = src[...].astype(BF16)
    if chunked_lhs:
        lhs_b_ref, = scratch
        tm, k = lhs_b_ref.shape

        @pl.when(j == 0)
        def _cast():
            lhs_b_ref[...] = lhs_ref[...].reshape(tm, k).astype(BF16)

        lhs_parts = [lhs_b_ref[...]]
    else:
        lhs_parts = [r[...] for r in lhs_refs]
    accs = []
    for ws in w_refs:
        acc = jnp.dot(lhs_parts[0], ws[0][...].astype(BF16), preferred_element_type=F32)
        for part, w in zip(lhs_parts[1:], ws[1:]):
            acc = acc + jnp.dot(part, w[...].astype(BF16), preferred_element_type=F32)
        accs.append(acc)
    if epilogue == "plain":
        out = accs[0]
    elif epilogue == "glu":
        zt = lhs_ref[:, :, pl.ds(pl.multiple_of(j * tn, tn), tn)]
        out = zt.reshape(accs[0].shape) * jax.nn.sigmoid(accs[0])
    elif epilogue == "swiglu":
        a1, a3 = accs
        out = a1 * jax.nn.sigmoid(a1) * a3
    elif epilogue == "conv":
        gb, gc, v = accs
        cw_ref, = extra
        t = gc * v
        tm = t.shape[0]
        pos = lax.broadcasted_iota(jnp.int32, t.shape, 0) % GRID_W
        prev = jnp.where(pos == 0, 0.0, pltpu.roll(t, 1, 0))
        nxt = jnp.where(pos == GRID_W - 1, 0.0, pltpu.roll(t, tm - 1, 0))
        out = gb * (cw_ref[0:1, :] * prev + cw_ref[1:2, :] * t + cw_ref[2:3, :] * nxt)
    else:
        raise ValueError(epilogue)
    out = out.astype(o_ref.dtype)
    if out_batches is None:
        o_ref[...] = out
    elif out_batches == 1:
        o_ref[...] = out.reshape(o_ref.shape)
    else:
        n_tok = out.shape[0] // out_batches
        for bb in range(out_batches):
            o_ref[:, bb, :, :] = out[bb * n_tok:(bb + 1) * n_tok].reshape(n_tok // T_CHUNK, T_CHUNK, tn)


def _matmul(lhs, weights, n_out, *, tm, tn, epilogue, out_dtype, chunked_lhs=False, chunk_tokens=None,
            extra=(), extra_specs=(), single_buffer_lhs=False, split_lhs=False, side_casts=(), name="matmul"):
    k = weights[0][0].shape[1]
    buffered = dict(pipeline_mode=pl.Buffered(1)) if single_buffer_lhs else {}
    scratch_shapes = []
    n_k = 2 if split_lhs else 1
    kh = k // n_k
    if chunked_lhs:
        jn, bn, tt, _ = lhs.shape
        m = jn * bn * tt
        per = (jn * tt) // tm
        lhs_specs = [pl.BlockSpec((tm // tt, None, tt, k), lambda i, j: (i % per, i // per, 0, 0), **buffered)]
        scratch_shapes.append(pltpu.VMEM((tm, k), BF16))
    elif split_lhs:
        m = lhs.shape[0]
        lhs_specs = [pl.BlockSpec((tm, kh), lambda i, j: (i, 0)),
                     pl.BlockSpec((tm, kh), lambda i, j: (i, 1), pipeline_mode=pl.Buffered(1))]
    else:
        m = lhs.shape[0]
        lhs_specs = [pl.BlockSpec((tm, k), lambda i, j: (i, 0), **buffered)]
    w_specs = [pl.BlockSpec((None, kh, tn), functools.partial(lambda i, j, layer, blk, kk: (layer, kk, j + blk),
                                                              layer=layer, blk=off // tn, kk=kk))
               for (_, layer, off) in weights for kk in range(n_k)]
    out_batches = None
    if chunk_tokens is not None:
        jn = chunk_tokens // T_CHUNK
        if tm <= chunk_tokens:
            out_batches = 1
            per_o = chunk_tokens // tm
            out_spec = pl.BlockSpec((tm // T_CHUNK, None, T_CHUNK, tn), lambda i, j: (i % per_o, i // per_o, 0, j))
        else:
            out_batches = tm // chunk_tokens
            out_spec = pl.BlockSpec((jn, out_batches, T_CHUNK, tn), lambda i, j: (0, i, 0, j))
        out_shape = jax.ShapeDtypeStruct((jn, BATCH, T_CHUNK, n_out), out_dtype)
    else:
        out_spec = pl.BlockSpec((tm, tn), lambda i, j: (i, j))
        out_shape = jax.ShapeDtypeStruct((m, n_out), out_dtype)
    w_arrays = [w for (w, _, _) in weights for _ in range(n_k)]
    lhs_bufs = 1.5 if split_lhs else (1 if single_buffer_lhs else 2)
    lhs_bytes = int(tm * k * lhs.dtype.itemsize * lhs_bufs) + (tm * k * 2 if chunked_lhs else 0)
    w_bytes = sum(2 * k * tn * w.dtype.itemsize + (k * tn * 2 if w.dtype != BF16 else 0) for (w, _, _) in weights)
    out_bytes = tm * tn * (2 * jnp.dtype(out_dtype).itemsize + 4 * (len(weights) + 3))
    n_j = n_out // tn
    side_in_specs, side_out_specs, side_out_shapes, side_srcs, side_bytes = [], [], [], [], 0
    for src, rows, first, n_blocks in side_casts:
        cols = src.shape[1]

        def blk(i, j, n_blocks=n_blocks):
            return jnp.minimum(i * n_j + j, n_blocks - 1)

        side_in_specs.append(pl.BlockSpec((rows, cols), lambda i, j, blk=blk, first=first: (blk(i, j) + first, 0)))
        side_out_specs.append(pl.BlockSpec((rows, cols), lambda i, j, blk=blk: (blk(i, j), 0)))
        side_out_shapes.append(jax.ShapeDtypeStruct((rows * n_blocks, cols), BF16))
        side_srcs.append(src)
        side_bytes += 2 * rows * cols * 6
    res = pl.pallas_call(
        functools.partial(_mm_kernel, n_w=len(weights), epilogue=epilogue, chunked_lhs=chunked_lhs,
                          out_batches=out_batches, tn=tn, n_side=len(side_srcs), n_k=n_k),
        grid=(m // tm, n_j),
        in_specs=lhs_specs + w_specs + list(extra_specs) + side_in_specs,
        out_specs=[out_spec] + side_out_specs, out_shape=[out_shape] + side_out_shapes,
        scratch_shapes=scratch_shapes,
        compiler_params=_cparams(("arbitrary", "arbitrary"),
                                 lhs_bytes + w_bytes + out_bytes + side_bytes + (4 << 20)),
        name=name,
    )(*([lhs] * n_k), *w_arrays, *extra, *side_srcs)
    return res if side_srcs else res[0]


def _s5_prep_kernel(lr_ref, li_ref, ls_ref, bre_ref, bim_ref, cre_ref, cim_ref,
                    tall_ref, w_ref, v_ref, at_ref, *, reverse):
    lr = lr_ref[...]
    li = li_ref[...]
    dt = jnp.exp(ls_ref[...])
    kk = jnp.minimum(lax.broadcasted_iota(jnp.int32, (POW_ROWS, SP), 0), T_CHUNK).astype(F32)
    mag = jnp.exp(kk * (lr * dt))
    ang = kk * (li * dt)
    p_re = mag * jnp.cos(ang)
    p_im = mag * jnp.sin(ang)
    a_re = p_re[1:2, :]
    a_im = p_im[1:2, :]
    n_re, n_im = a_re - 1.0, a_im
    den = lr * lr + li * li
    f_re = (n_re * lr + n_im * li) / den
    f_im = (n_im * lr - n_re * li) / den
    sel_r = lax.broadcasted_iota(jnp.int32, (STATE, SP), 0)
    sel_c = lax.broadcasted_iota(jnp.int32, (STATE, SP), 1)
    spread = ((sel_c & (STATE - 1)) == sel_r).astype(F32)
    blk_r = lax.broadcasted_iota(jnp.int32, (LANES, SP), 0) // GROUP
    blk_c = lax.broadcasted_iota(jnp.int32, (LANES, SP), 1) // STATE
    on_diag = blk_r == blk_c

    def blockdiag(ref):
        full = jnp.dot(ref[...], spread, preferred_element_type=F32, precision=lax.Precision.HIGHEST)
        return jnp.where(on_diag, full, 0.0)

    b_re = blockdiag(bre_ref)
    b_im = blockdiag(bim_ref)
    bb_re = f_re * b_re - f_im * b_im
    bb_im = f_re * b_im + f_im * b_re
    c_re = blockdiag(cre_ref).T
    c_im = blockdiag(cim_ref).T
    c_re_b = c_re.astype(BF16)
    c_im_b = c_im.astype(BF16)

    dks = []
    for k in range(T_CHUNK):
        pr, pi = p_re[k:k + 1, :], p_im[k:k + 1, :]
        bk_re = (bb_re * pr - bb_im * pi).astype(BF16)
        bk_im = (bb_re * pi + bb_im * pr).astype(BF16)
        dk = (jnp.dot(bk_re, c_re_b, preferred_element_type=F32)
              - jnp.dot(bk_im, c_im_b, preferred_element_type=F32))
        dks.append(dk.astype(BF16))
        m = k if reverse else T_CHUNK - 1 - k
        w_ref[m * LANES:(m + 1) * LANES, 0:SP] = bk_re
        w_ref[m * LANES:(m + 1) * LANES, SP:2 * SP] = bk_im

    zero = jnp.zeros((LANES, LANES), BF16)
    for r in range(T_CHUNK):
        for c in range(2):
            idx = (r - c) if reverse else (T_CHUNK - 2 - r + c)
            blk = dks[idx] if 0 <= idx < T_CHUNK else zero
            tall_ref[r * LANES:(r + 1) * LANES, c * LANES:(c + 1) * LANES] = blk

    pad = jnp.zeros((LANES - POW_ROWS, SP), F32)
    pt_re = jnp.concatenate([p_re, pad], axis=0).T
    pt_im = jnp.concatenate([p_im, pad], axis=0).T
    for t in range(T_CHUNK):
        k = (T_CHUNK - t) if reverse else (t + 1)
        pr, pi = pt_re[:, k:k + 1], pt_im[:, k:k + 1]
        v_ref[0:SP, t * LANES:(t + 1) * LANES] = (pr * c_re - pi * c_im).astype(BF16)
        v_ref[SP:2 * SP, t * LANES:(t + 1) * LANES] = (-(pr * c_im + pi * c_re)).astype(BF16)

    at_ref[:, 0:SP] = p_re[T_CHUNK:T_CHUNK + 1, :]
    at_ref[:, SP:2 * SP] = p_im[T_CHUNK:T_CHUNK + 1, :]


def _s5_prep(lam_re, lam_im, log_step, b_re, b_im, c_re, c_im, *, reverse):
    def rows(v):
        return v.reshape(N_SLAB, 1, SP)

    def b_compact(b):
        return b.reshape(N_SLAB, G_SLAB, STATE, GROUP).transpose(0, 1, 3, 2).reshape(N_SLAB, LANES, STATE)

    def c_compact(c):
        return c.reshape(N_SLAB, LANES, STATE)

    ls = jnp.repeat(log_step, STATE).reshape(N_SLAB, 1, SP)
    vec = pl.BlockSpec((None, 1, SP), lambda s: (s, 0, 0))
    cmp_spec = pl.BlockSpec((None, LANES, STATE), lambda s: (s, 0, 0))
    return pl.pallas_call(
        functools.partial(_s5_prep_kernel, reverse=reverse),
        grid=(N_SLAB,),
        in_specs=[vec, vec, vec, cmp_spec, cmp_spec, cmp_spec, cmp_spec],
        out_specs=[pl.BlockSpec((None, KT, 2 * LANES), lambda s: (s, 0, 0)),
                   pl.BlockSpec((None, KT, 2 * SP), lambda s: (s, 0, 0)),
                   pl.BlockSpec((None, 2 * SP, KT), lambda s: (s, 0, 0)),
                   pl.BlockSpec((None, 1, 2 * SP), lambda s: (s, 0, 0))],
        out_shape=[jax.ShapeDtypeStruct((N_SLAB, KT, 2 * LANES), BF16),
                   jax.ShapeDtypeStruct((N_SLAB, KT, 2 * SP), BF16),
                   jax.ShapeDtypeStruct((N_SLAB, 2 * SP, KT), BF16),
                   jax.ShapeDtypeStruct((N_SLAB, 1, 2 * SP), F32)],
        compiler_params=_cparams(("arbitrary",), 40 << 20),
        name="s5_prep_bwd" if reverse else "s5_prep_fwd",
    )(rows(lam_re), rows(lam_im), ls, b_compact(b_re), b_compact(b_im), c_compact(c_re), c_compact(c_im))


def _s5_scan_kernel(*refs, reverse, second):
    it = iter(refs)
    ul_ref = next(it)
    uc_ref = next(it)
    y1_ref = next(it) if second else None
    tall_ref, w_ref, v_ref, at_ref = next(it), next(it), next(it), next(it)
    dsk_ref = None if second else next(it)
    o_ref = next(it)
    x2_ref, xc2_ref, xst_ref, sprev_ref, state_ref = it

    rows = JB * BATCH
    crow = J_CTX * BATCH
    a_re = at_ref[:, 0:SP]
    a_im = at_ref[:, SP:2 * SP]

    def token(ref, t, n):
        return ref[pl.ds(t, n, stride=T_CHUNK), :]

    def scan(n_chunks, keep):
        s_re = state_ref[:, 0:SP]
        s_im = state_ref[:, SP:2 * SP]
        order = range(n_chunks - 1, -1, -1) if reverse else range(n_chunks)
        for jj in order:
            r = slice(jj * BATCH, (jj + 1) * BATCH)
            if keep:
                sprev_ref[r, 0:SP] = s_re
                sprev_ref[r, SP:2 * SP] = s_im
            x_re = xst_ref[r, 0:SP]
            x_im = xst_ref[r, SP:2 * SP]
            s_re, s_im = (a_re * s_re - a_im * s_im + x_re,
                          a_re * s_im + a_im * s_re + x_im)
        state_ref[:, 0:SP] = s_re
        state_ref[:, SP:2 * SP] = s_im

    @pl.when(pl.program_id(1) == 0)
    def _context():
        state_ref[...] = jnp.zeros_like(state_ref)
        for t in range(T_CHUNK):
            xc2_ref[:, t * LANES:(t + 1) * LANES] = token(uc_ref, t, crow).astype(BF16)
        xst_ref[0:crow, :] = jnp.dot(xc2_ref[...], w_ref[...], preferred_element_type=F32)
        scan(J_CTX, keep=False)

    for t in range(T_CHUNK):
        x2_ref[:, t * LANES:(t + 1) * LANES] = token(ul_ref, t, rows).astype(BF16)
    xst_ref[...] = jnp.dot(x2_ref[...], w_ref[...], preferred_element_type=F32)
    scan(JB, keep=True)
    sp = sprev_ref[...].astype(BF16)

    n_tiles = KT // (2 * LANES)
    for nt in range(n_tiles):
        cols = slice(nt * 2 * LANES, (nt + 1) * 2 * LANES)
        if reverse:
            xa = x2_ref[:, nt * 2 * LANES:]
            ta = tall_ref[0:(n_tiles - nt) * 2 * LANES, :]
        else:
            xa = x2_ref[:, 0:(nt + 1) * 2 * LANES]
            ta = tall_ref[(n_tiles - 1 - nt) * 2 * LANES:, :]
        y = (jnp.dot(xa, ta, preferred_element_type=F32)
             + jnp.dot(sp, v_ref[:, cols], preferred_element_type=F32))
        for c in range(2):
            t = 2 * nt + c
            yt = y[:, c * LANES:(c + 1) * LANES]
            if second:
                val = jax.nn.gelu(token(y1_ref, t, rows) + yt)
            else:
                val = dsk_ref[...] * token(ul_ref, t, rows) + yt
            o_ref[pl.ds(t, rows, stride=T_CHUNK), :] = val


def _s5_scan(u_lat, u_ctx, ops, *, reverse, y_prev=None, d_skip=None):
    tall, w, v, at = ops
    second = y_prev is not None
    blk_rows = JB * BATCH * T_CHUNK
    ctx_rows = J_CTX * BATCH * T_CHUNK

    def jblk(q):
        return (N_JB - 1 - q) if reverse else q

    lat_spec = pl.BlockSpec((blk_rows, LANES), lambda s, q: (jblk(q), s))
    in_specs = [lat_spec, pl.BlockSpec((ctx_rows, LANES), lambda s, q: (0, s))]
    args = [u_lat.reshape(J_LAT * BATCH * T_CHUNK, D_MODEL), u_ctx.reshape(ctx_rows, D_MODEL)]
    if second:
        in_specs.append(lat_spec)
        args.append(y_prev.reshape(J_LAT * BATCH * T_CHUNK, D_MODEL))
    in_specs += [pl.BlockSpec((None, KT, 2 * LANES), lambda s, q: (s, 0, 0)),
                 pl.BlockSpec((None, KT, 2 * SP), lambda s, q: (s, 0, 0)),
                 pl.BlockSpec((None, 2 * SP, KT), lambda s, q: (s, 0, 0)),
                 pl.BlockSpec((None, 1, 2 * SP), lambda s, q: (s, 0, 0))]
    args += [tall, w, v, at]
    if not second:
        in_specs.append(pl.BlockSpec((None, None, 1, LANES), lambda s, q: (0, s, 0, 0)))
        args.append(d_skip.reshape(1, N_SLAB, 1, LANES))
    rows = JB * BATCH
    out = pl.pallas_call(
        functools.partial(_s5_scan_kernel, reverse=reverse, second=second),
        grid=(N_SLAB, N_JB),
        in_specs=in_specs,
        out_specs=lat_spec,
        out_shape=jax.ShapeDtypeStruct((J_LAT * BATCH * T_CHUNK, D_MODEL), F32),
        scratch_shapes=[pltpu.VMEM((rows, KT), BF16),
                        pltpu.VMEM((J_CTX * BATCH, KT), BF16),
                        pltpu.VMEM((rows, 2 * SP), F32),
                        pltpu.VMEM((rows, 2 * SP), F32),
                        pltpu.VMEM((BATCH, 2 * SP), F32)],
        compiler_params=_cparams(("arbitrary", "arbitrary"), VMEM_CEILING_BYTES),
        name="s5_scan_bwd" if reverse else "s5_scan_fwd",
    )(*args)
    return out.reshape(J_LAT, BATCH, T_CHUNK, D_MODEL)


def kernel(x, c, ctx, c_ctx, w_mod, b_mod, ln_g, ln_b, w_in_a, lam_re, lam_im, log_step,
           b_re, b_im, c_re, c_im, d_skip, w_glu_a, w_out_a, w_in_b, conv_w, w_out_b, w1, w3, w2):
    d = D_MODEL
    d_ff = w1.shape[-1]
    m = BATCH * SEQ

    c_rows = jnp.concatenate([c, c_ctx[None, :], jnp.zeros((MOD_ROWS - BATCH - 1, d), F32)], axis=0)
    mods = [_mod_rows(c_rows, w_mod, b_mod, i) for i in range(DEPTH)]

    def w2_resid(g, w2_any, layer, stream):
        return _matmul_resid(g, w2_any, layer, stream, mods[layer], 5, tm=1024, tn=512, prev_ln=(layer, 0),
                             ln_g=ln_g, ln_b=ln_b, name="mm_w2")

    h_l = _modulate(x, mods[0], 0, 1)
    h_c = _modulate(ctx, mods[0], 0, 1, fixed_row=BATCH)
    u_l = _matmul(h_l, [(w_in_a, 0, 0)], d, tm=2048, tn=256, epilogue="plain", out_dtype=F32,
                  chunk_tokens=SEQ, split_lhs=True, name="mm_in_a")
    u_c = _matmul(h_c, [(w_in_a, 0, 0)], d, tm=BATCH * CTX_LEN, tn=512, epilogue="plain", out_dtype=F32,
                  chunk_tokens=CTX_LEN, single_buffer_lhs=True, name="mm_in_a_ctx")
    ops_f = _s5_prep(lam_re[0, 0], lam_im[0, 0], log_step[0, 0], b_re[0, 0], b_im[0, 0],
                     c_re[0, 0], c_im[0, 0], reverse=False)
    ops_b = _s5_prep(lam_re[0, 1], lam_im[0, 1], log_step[0, 1], b_re[0, 1], b_im[0, 1],
                     c_re[0, 1], c_im[0, 1], reverse=True)
    y_f = _s5_scan(u_l, u_c, ops_f, reverse=False, d_skip=d_skip)
    z = _s5_scan(u_l, u_c, ops_b, reverse=True, y_prev=y_f)
    gz = _matmul(z, [(w_glu_a, 0, 0)], d, tm=1024, tn=512, epilogue="glu", out_dtype=BF16,
                 chunked_lhs=True, single_buffer_lhs=True, name="mm_glu")
    y = _matmul_resid(gz, w_out_a, 0, x.reshape(m, d), mods[0], 2, tm=2048, tn=256, split_lhs=True,
                      name="mm_out_a")

    h, mu, rs = _norm(y, ln_g, ln_b, 0, 0, mods[0], 3, 4)
    n_steps = (m // 2048) * (d_ff // 256)
    blk16 = d // 16
    side = [(w2.reshape(DEPTH * d_ff, d), DEPTH * d_ff // n_steps, 0, n_steps),
            (w_in_b.reshape(d, 3 * d), 16, 0, blk16),
            (w_out_b.reshape(d, d), 16, 0, blk16),
            (w1.reshape(DEPTH * d, d_ff), 16, blk16, blk16),
            (w3.reshape(DEPTH * d, d_ff), 16, blk16, blk16)]
    g, w2_b, w_in_b_b, w_out_b_b, w1_b, w3_b = _matmul(
        h, [(w1, 0, 0), (w3, 0, 0)], d_ff, tm=2048, tn=256, epilogue="swiglu", out_dtype=BF16,
        split_lhs=True, side_casts=side, name="mm_swiglu")
    w2_b = w2_b.reshape(DEPTH, d_ff, d)
    w_in_b_b = w_in_b_b.reshape(1, d, 3 * d)
    w_out_b_b = w_out_b_b.reshape(1, d, d)
    w1_b = w1_b.reshape(1, d, d_ff)
    w3_b = w3_b.reshape(1, d, d_ff)
    y = w2_resid(g, w2_b, 0, (y, mu, rs))

    h, mu, rs = _norm(y, ln_g, ln_b, 0, 1, mods[1], 0, 1)
    cw_spec = pl.BlockSpec((None, 3, 256), lambda i, j: (0, 0, j))
    gz = _matmul(h, [(w_in_b_b, 0, 0), (w_in_b_b, 0, d), (w_in_b_b, 0, 2 * d)], d, tm=1024, tn=256,
                 epilogue="conv", out_dtype=BF16, extra=(conv_w,), extra_specs=(cw_spec,), name="mm_conv")
    y = _matmul_resid(gz, w_out_b_b, 0, (y, mu, rs), mods[1], 2, tm=2048, tn=256, prev_ln=(0, 1),
                      ln_g=ln_g, ln_b=ln_b, split_lhs=True, name="mm_out_b")
    h, mu, rs = _norm(y, ln_g, ln_b, 1, 0, mods[1], 3, 4)
    g = _matmul(h, [(w1_b, 0, 0), (w3_b, 0, 0)], d_ff, tm=2048, tn=256, epilogue="swiglu", out_dtype=BF16,
                name="mm_swiglu")
    y = w2_resid(g, w2_b, 1, (y, mu, rs))
    return _norm(y, ln_g, ln_b, 1, 1).reshape(BATCH, SEQ, d)
```
